```python
import math
import jax, jax.numpy as jnp
from jax import lax
import numpy as np

D_MODEL = 1024
BATCH = 4
SEQ = 8192
DEPTH = 1
DEC_BATCH = 2
DEC_SEQ = 8192
PAST_LEN = 128

D_FF = 2816
SSM_HEADS = 16
SSM_HEAD_DIM = 64
D_SSM = SSM_HEADS * SSM_HEAD_DIM
SSM_GROUPS = 2
D_STATE = 128
CONV_K = 5
CONV_DIM = D_SSM + 2 * SSM_GROUPS * D_STATE
CHUNK = 128
N_HEADS = 16
KV_HEADS = 4
HEAD_DIM = 64
D_ATTN = N_HEADS * HEAD_DIM
D_KV = KV_HEADS * HEAD_DIM
WINDOW = 128
BLK = 128
D_MIX = D_SSM + D_ATTN
IN_SPLITS = (D_SSM, CONV_DIM, SSM_HEADS, SSM_HEADS, D_ATTN, D_KV, D_KV)
D_IN_PROJ = sum(IN_SPLITS)
EPS = 1e-6

kernel_name = "hymba_bidir_ssd_swa_macaron_encoder"


def rmsnorm(x, w):
    xf = x.astype(jnp.float32)
    y = xf * lax.rsqrt(jnp.mean(xf * xf, axis=-1, keepdims=True) + EPS)
    return (y * w.astype(jnp.float32)).astype(x.dtype)


def swiglu(x, w_gate, w_up, w_down):
    return (jax.nn.silu(x @ w_gate) * (x @ w_up)) @ w_down


def centred_depthwise_conv(u, w, b):
    S = u.shape[1]
    pad = (CONV_K - 1) // 2
    up = jnp.pad(u, ((0, 0), (pad, pad), (0, 0)))
    out = b
    for k in range(CONV_K):
        out = out + up[:, k:k + S] * w[k]
    return out


def ssd_scan(xh, dt, a, Bm, Cm):
    b, S, h, p = xh.shape
    nc = S // CHUNK
    g = SSM_GROUPS
    hg = h // g
    xdt = (xh * dt[..., None]).reshape(b, nc, CHUNK, g, hg, p)
    dA = (dt * a).reshape(b, nc, CHUNK, g, hg)
    Bc = Bm.reshape(b, nc, CHUNK, g, D_STATE)
    Cc = Cm.reshape(b, nc, CHUNK, g, D_STATE)
    cs = jnp.cumsum(dA, axis=2)
    seg = cs[:, :, :, None] - cs[:, :, None, :]
    tril = jnp.tril(jnp.ones((CHUNK, CHUNK), dtype=bool))[None, None, :, :, None, None]
    Lmat = jnp.exp(jnp.where(tril, seg, -jnp.inf))
    CB = jnp.einsum('bclgn,bcsgn->bclsg', Cc, Bc)
    y_diag = jnp.einsum('bclsgj,bcsgjp->bclgjp', CB[..., None] * Lmat, xdt)
    decay = jnp.exp(cs[:, :, -1:] - cs)
    states = jnp.einsum('bclgn,bclgjp->bcgjpn', Bc, decay[..., None] * xdt)
    chunk_decay = jnp.exp(cs[:, :, -1])

    def step(carry, inp):
        st, dec = inp
        return carry * dec[..., None, None] + st, carry

    init = jnp.zeros((b, g, hg, p, D_STATE), jnp.float32)
    _, prev = lax.scan(step, init, (jnp.swapaxes(states, 0, 1), jnp.swapaxes(chunk_decay, 0, 1)))
    prev = jnp.swapaxes(prev, 0, 1)
    y_off = jnp.einsum('bclgn,bcgjpn->bclgjp', Cc, prev) * jnp.exp(cs)[..., None]
    return (y_diag + y_off).reshape(b, S, h, p)


def ssd_mixer(z, xBC, dt_f_raw, dt_b_raw, conv_w, conv_b, dt_bias_fwd, dt_bias_bwd,
              a_log_fwd, a_log_bwd, d_skip, ssm_norm_w):
    b, S, _ = z.shape
    u = jax.nn.silu(centred_depthwise_conv(xBC, conv_w, conv_b)).astype(jnp.float32)
    xs, Bm, Cm = jnp.split(u, [D_SSM, D_SSM + SSM_GROUPS * D_STATE], axis=-1)
    xh = xs.reshape(b, S, SSM_HEADS, SSM_HEAD_DIM)
    Bm = Bm.reshape(b, S, SSM_GROUPS, D_STATE)
    Cm = Cm.reshape(b, S, SSM_GROUPS, D_STATE)
    dt_f = jax.nn.softplus(dt_f_raw.astype(jnp.float32) + dt_bias_fwd.astype(jnp.float32))
    dt_b = jax.nn.softplus(dt_b_raw.astype(jnp.float32) + dt_bias_bwd.astype(jnp.float32))
    a_f = -jnp.exp(a_log_fwd.astype(jnp.float32))
    a_b = -jnp.exp(a_log_bwd.astype(jnp.float32))
    y_f = ssd_scan(xh, dt_f, a_f, Bm, Cm)
    y_b = jnp.flip(ssd_scan(jnp.flip(xh, 1), jnp.flip(dt_b, 1), a_b,
                            jnp.flip(Bm, 1), jnp.flip(Cm, 1)), 1)
    y = y_f + y_b + xh * d_skip.astype(jnp.float32)[:, None]
    y = y.reshape(b, S, D_SSM) * jax.nn.silu(z.astype(jnp.float32))
    yg = y.reshape(b, S, SSM_GROUPS, D_SSM // SSM_GROUPS)
    yg = yg * lax.rsqrt(jnp.mean(yg * yg, axis=-1, keepdims=True) + EPS)
    y = yg.reshape(b, S, D_SSM) * ssm_norm_w.astype(jnp.float32)
    return y.astype(z.dtype)


def windowed_gqa(q, k, v, attn_sink):
    b, S, _ = q.shape
    G = N_HEADS // KV_HEADS
    nblk = S // BLK
    qh = q.reshape(b, S, KV_HEADS, G, HEAD_DIM)
    kp = jnp.pad(k.reshape(b, S, KV_HEADS, HEAD_DIM), ((0, 0), (BLK, BLK), (0, 0), (0, 0)))
    vp = jnp.pad(v.reshape(b, S, KV_HEADS, HEAD_DIM), ((0, 0), (BLK, BLK), (0, 0), (0, 0)))
    slopes = jnp.exp2(-(8.0 / N_HEADS) * jnp.arange(1, N_HEADS + 1, dtype=jnp.float32))
    slopes = slopes.reshape(KV_HEADS, G)[None, :, :, None, None]
    sink = attn_sink.astype(jnp.float32).reshape(KV_HEADS, G)[None, :, :, None, None]
    scale = 1.0 / math.sqrt(HEAD_DIM)

    def block(i):
        qs = i * BLK
        qb = lax.dynamic_slice_in_dim(qh, qs, BLK, axis=1)
        kb = lax.dynamic_slice_in_dim(kp, qs, 3 * BLK, axis=1)
        vb = lax.dynamic_slice_in_dim(vp, qs, 3 * BLK, axis=1)
        s = jnp.einsum('bqkgd,bskd->bkgqs', qb, kb,
                       preferred_element_type=jnp.float32) * scale
        tpos = qs + jnp.arange(BLK)
        spos = qs - BLK + jnp.arange(3 * BLK)
        dist = jnp.abs(tpos[:, None] - spos[None, :])
        valid = (dist <= WINDOW) & (spos >= 0)[None, :] & (spos < S)[None, :]
        s = s - slopes * dist.astype(jnp.float32)
        s = jnp.where(valid, s, -jnp.inf)
        mx = jnp.maximum(jnp.max(s, axis=-1, keepdims=True), sink)
        p = jnp.exp(s - mx)
        den = jnp.sum(p, axis=-1, keepdims=True) + jnp.exp(sink - mx)
        o = jnp.einsum('bkgqs,bskd->bqkgd', p / den, vb.astype(jnp.float32))
        return o

    out = lax.map(block, jnp.arange(nblk))
    out = jnp.swapaxes(out, 0, 1).reshape(b, S, D_ATTN)
    return out.astype(q.dtype)


def parallel_mixer(h, w_in, conv_w, conv_b, dt_bias_fwd, dt_bias_bwd, a_log_fwd, a_log_bwd,
                   d_skip, ssm_norm_w, attn_sink, w_out):
    proj = h @ w_in
    idx = list(np.cumsum(IN_SPLITS)[:-1])
    z, xBC, dt_f, dt_b, q, k, v = jnp.split(proj, idx, axis=-1)
    y_ssm = ssd_mixer(z, xBC, dt_f, dt_b, conv_w, conv_b, dt_bias_fwd, dt_bias_bwd,
                      a_log_fwd, a_log_bwd, d_skip, ssm_norm_w)
    y_att = windowed_gqa(q, k, v, attn_sink)
    return jnp.concatenate([y_ssm, y_att], axis=-1) @ w_out


def trunk(x, norm_ffn1_w, ffn1_w_gate, ffn1_w_up, ffn1_w_down, norm_mix_w, w_in, conv_w, conv_b,
          dt_bias_fwd, dt_bias_bwd, a_log_fwd, a_log_bwd, d_skip, ssm_norm_w, attn_sink, w_out,
          norm_ffn2_w, ffn2_w_gate, ffn2_w_up, ffn2_w_down, norm_final_w):
    for l in range(DEPTH):
        x = x + 0.5 * swiglu(rmsnorm(x, norm_ffn1_w[l]), ffn1_w_gate[l], ffn1_w_up[l], ffn1_w_down[l])
        x = x + parallel_mixer(rmsnorm(x, norm_mix_w[l]), w_in[l], conv_w[l], conv_b[l],
                               dt_bias_fwd[l], dt_bias_bwd[l], a_log_fwd[l], a_log_bwd[l],
                               d_skip[l], ssm_norm_w[l], attn_sink[l], w_out[l])
        x = x + 0.5 * swiglu(rmsnorm(x, norm_ffn2_w[l]), ffn2_w_gate[l], ffn2_w_up[l], ffn2_w_down[l])
    return rmsnorm(x, norm_final_w)


def setup_inputs(seed: int = 0) -> dict:
    key = jax.random.key(seed)
    ks = jax.random.split(key, 24)
    f32 = jnp.float32

    def nrm(k, shape, scale):
        return jax.random.normal(k, shape, f32) * scale

    def gain(k, shape):
        return 1.0 + 0.02 * jax.random.normal(k, shape, f32)

    dt_init_f = jnp.exp(jax.random.uniform(ks[10], (DEPTH, SSM_HEADS), f32, math.log(1e-3), math.log(1e-1)))
    dt_init_b = jnp.exp(jax.random.uniform(ks[11], (DEPTH, SSM_HEADS), f32, math.log(1e-3), math.log(1e-1)))
    return {
        "x_prompt": nrm(ks[0], (BATCH, SEQ, D_MODEL), 1.0),
        "x_sample": nrm(ks[1], (DEC_BATCH, DEC_SEQ, D_MODEL), 1.0),
        "norm_ffn1_w": gain(ks[2], (DEPTH, D_MODEL)),
        "ffn1_w_gate": nrm(ks[3], (DEPTH, D_MODEL, D_FF), D_MODEL ** -0.5),
        "ffn1_w_up": nrm(ks[4], (DEPTH, D_MODEL, D_FF), D_MODEL ** -0.5),
        "ffn1_w_down": nrm(ks[5], (DEPTH, D_FF, D_MODEL), D_FF ** -0.5),
        "norm_mix_w": gain(ks[6], (DEPTH, D_MODEL)),
        "w_in": nrm(ks[7], (DEPTH, D_MODEL, D_IN_PROJ), D_MODEL ** -0.5),
        "conv_w": nrm(ks[8], (DEPTH, CONV_K, CONV_DIM), CONV_K ** -0.5),
        "conv_b": nrm(ks[9], (DEPTH, CONV_DIM), 0.02),
        "dt_bias_fwd": dt_init_f + jnp.log(-jnp.expm1(-dt_init_f)),
        "dt_bias_bwd": dt_init_b + jnp.log(-jnp.expm1(-dt_init_b)),
        "a_log_fwd": jnp.log(jax.random.uniform(ks[12], (DEPTH, SSM_HEADS), f32, 1.0, 16.0)),
        "a_log_bwd": jnp.log(jax.random.uniform(ks[13], (DEPTH, SSM_HEADS), f32, 1.0, 16.0)),
        "d_skip": 1.0 + 0.1 * jax.random.normal(ks[14], (DEPTH, SSM_HEADS), f32),
        "ssm_norm_w": gain(ks[15], (DEPTH, D_SSM)),
        "attn_sink": nrm(ks[16], (DEPTH, N_HEADS), 0.5),
        "w_out": nrm(ks[17], (DEPTH, D_MIX, D_MODEL), D_MIX ** -0.5),
        "norm_ffn2_w": gain(ks[18], (DEPTH, D_MODEL)),
        "ffn2_w_gate": nrm(ks[19], (DEPTH, D_MODEL, D_FF), D_MODEL ** -0.5),
        "ffn2_w_up": nrm(ks[20], (DEPTH, D_MODEL, D_FF), D_MODEL ** -0.5),
        "ffn2_w_down": nrm(ks[21], (DEPTH, D_FF, D_MODEL), D_FF ** -0.5),
        "norm_final_w": gain(ks[22], (D_MODEL,)),
    }


def reference(x_prompt, x_sample, norm_ffn1_w, ffn1_w_gate, ffn1_w_up, ffn1_w_down, norm_mix_w,
              w_in, conv_w, conv_b, dt_bias_fwd, dt_bias_bwd, a_log_fwd, a_log_bwd, d_skip,
              ssm_norm_w, attn_sink, w_out, norm_ffn2_w, ffn2_w_gate, ffn2_w_up, ffn2_w_down,
              norm_final_w):
    y_prompt = trunk(x_prompt, norm_ffn1_w, ffn1_w_gate, ffn1_w_up, ffn1_w_down, norm_mix_w, w_in,
                     conv_w, conv_b, dt_bias_fwd, dt_bias_bwd, a_log_fwd, a_log_bwd, d_skip,
                     ssm_norm_w, attn_sink, w_out, norm_ffn2_w, ffn2_w_gate, ffn2_w_up,
                     ffn2_w_down, norm_final_w)
    y_sample = trunk(x_sample, norm_ffn1_w, ffn1_w_gate, ffn1_w_up, ffn1_w_down, norm_mix_w, w_in,
                     conv_w, conv_b, dt_bias_fwd, dt_bias_bwd, a_log_fwd, a_log_bwd, d_skip,
                     ssm_norm_w, attn_sink, w_out, norm_ffn2_w, ffn2_w_gate, ffn2_w_up,
                     ffn2_w_down, norm_final_w)
    return (y_prompt, y_sample)
```

```python
import functools
import math

import jax
import jax.numpy as jnp
import numpy as np
from jax import lax
from jax.experimental import pallas as pl
from jax.experimental.pallas import tpu as pltpu

F32 = jnp.float32
BF16 = jnp.bfloat16

D_MODEL = 1024
D_FF = 2816
SSM_HEADS = 16
SSM_HEAD_DIM = 64
D_SSM = SSM_HEADS * SSM_HEAD_DIM
SSM_GROUPS = 2
D_STATE = 128
CONV_K = 5
CONV_DIM = D_SSM + 2 * SSM_GROUPS * D_STATE
N_HEADS = 16
KV_HEADS = 4
HEAD_DIM = 64
D_ATTN = N_HEADS * HEAD_DIM
D_KV = KV_HEADS * HEAD_DIM
WINDOW = 128
EPS = 1e-6

LANES = 128
SUBLANES = 8
VMEM_BYTES_V7X = 64 * 1024 * 1024

CHUNK = LANES
TOK_TILE = 512
SEQ_TILE = 512
CHUNKS_PER_STEP = SEQ_TILE // CHUNK
FF_CHUNK = D_FF // 2
HALO = SUBLANES
DT_PAD = LANES
HEADS_PER_GROUP = SSM_HEADS // SSM_GROUPS
GROUP_W = HEADS_PER_GROUP * SSM_HEAD_DIM

ATTN_HEAD_ORDER = (0, 4, 1, 5, 2, 6, 3, 7, 8, 12, 9, 13, 10, 14, 11, 15)


def _vmem_limit(resident_bytes):
    return int(min(VMEM_BYTES_V7X - 8 * 1024 * 1024, 2 * resident_bytes + 8 * 1024 * 1024))


def _resident(shape):
    nd = len(shape)
    return pl.BlockSpec(shape, lambda *_: (0,) * nd, pipeline_mode=pl.Buffered(1))


def _dot(a, b):
    return jnp.dot(a, b, preferred_element_type=F32)


def _dot_nt(a, b):
    return lax.dot_general(a, b, (((1,), (1,)), ((), ())), preferred_element_type=F32)


def _dot_tn(a, b):
    return lax.dot_general(a, b, (((0,), (0,)), ((), ())), preferred_element_type=F32)


def _rmsnorm(x, w):
    return x * lax.rsqrt(jnp.mean(x * x, axis=-1, keepdims=True) + EPS) * w


def _silu(x):
    return x * jax.nn.sigmoid(x)


def _softplus(x):
    return jnp.maximum(x, 0.0) + jnp.log1p(jnp.exp(-jnp.abs(x)))


def _split2(x):
    hi = x.astype(BF16)
    lo = (x - hi.astype(F32)).astype(BF16)
    return hi, lo


def _split3(x):
    hi = x.astype(BF16)
    r = x - hi.astype(F32)
    mid = r.astype(BF16)
    lo = (r - mid.astype(F32)).astype(BF16)
    return hi, mid, lo


def _swiglu(xn, wg_ref, wu_ref, wd_ref):
    acc = None
    for c in range(D_FF // FF_CHUNK):
        cols = slice(c * FF_CHUNK, (c + 1) * FF_CHUNK)
        g = _dot(xn, wg_ref[:, cols])
        u = _dot(xn, wu_ref[:, cols])
        part = _dot((_silu(g) * u).astype(BF16), wd_ref[cols, :])
        acc = part if acc is None else acc + part
    return acc


def _ffn1_kernel(x_ref, nw_ref, wg_ref, wu_ref, wd_ref, o_ref):
    x = x_ref[...]
    xn = _rmsnorm(x, nw_ref[...]).astype(BF16)
    o_ref[...] = x + 0.5 * _swiglu(xn, wg_ref, wu_ref, wd_ref)


def _ffn1(x2d, nw, wg, wu, wd):
    n = x2d.shape[0]
    tile = pl.BlockSpec((TOK_TILE, D_MODEL), lambda i: (i, 0))
    resident = 2 * 3 * D_MODEL * D_FF + 4 * 4 * TOK_TILE * D_MODEL + 3 * 4 * TOK_TILE * FF_CHUNK
    return pl.pallas_call(
        _ffn1_kernel,
        grid=(n // TOK_TILE,),
        in_specs=[tile, _resident((1, D_MODEL)), _resident((D_MODEL, D_FF)),
                  _resident((D_MODEL, D_FF)), _resident((D_FF, D_MODEL))],
        out_specs=tile,
        out_shape=jax.ShapeDtypeStruct((n, D_MODEL), F32),
        compiler_params=pltpu.CompilerParams(
            dimension_semantics=("parallel",), vmem_limit_bytes=_vmem_limit(resident)),
        name="ffn1",
    )(x2d, nw, wg, wu, wd)


def _inproj_kernel(x_ref, nw_ref, wz_ref, wx_ref, wdt_ref, wq_ref, wk_ref, wv_ref,
                   z_ref, xbc_ref, dt_ref, q_ref, k_ref, v_ref):
    h = _rmsnorm(x_ref[...], nw_ref[...]).astype(BF16)
    z_ref[...] = _dot(h, wz_ref[...]).astype(BF16)
    xbc_ref[...] = _dot(h, wx_ref[...])
    dt_ref[...] = _dot(h, wdt_ref[...])
    q_ref[...] = (_dot(h, wq_ref[...]) * (1.0 / math.sqrt(HEAD_DIM))).astype(BF16)
    k_ref[...] = _dot(h, wk_ref[...]).astype(BF16)
    v_ref[...] = _dot(h, wv_ref[...]).astype(BF16)


def _inproj(x2d, nw, wz, wx, wdt, wq, wk, wv):
    n = x2d.shape[0]
    widths = (D_SSM, CONV_DIM, DT_PAD, D_ATTN, D_KV, D_KV)
    dtypes = (BF16, F32, F32, BF16, BF16, BF16)

    def tile(w):
        return pl.BlockSpec((TOK_TILE, w), lambda i: (i, 0))

    resident = 2 * D_MODEL * sum(widths) + 2 * 4 * TOK_TILE * (D_MODEL + sum(widths))
    return pl.pallas_call(
        _inproj_kernel,
        grid=(n // TOK_TILE,),
        in_specs=[tile(D_MODEL), _resident((1, D_MODEL))] + [_resident((D_MODEL, w)) for w in widths],
        out_specs=[tile(w) for w in widths],
        out_shape=[jax.ShapeDtypeStruct((n, w), d) for w, d in zip(widths, dtypes)],
        compiler_params=pltpu.CompilerParams(
            dimension_semantics=("parallel",), vmem_limit_bytes=_vmem_limit(resident)),
        name="in_proj",
    )(x2d, nw, wz, wx, wdt, wq, wk, wv)


def _conv_kernel(xm_ref, xp_ref, xn_ref, cw_ref, cb_ref, u_ref, pad_ref):
    j = pl.program_id(1)
    last = pl.num_programs(1) - 1
    pad_ref[0:HALO, :] = jnp.where(j > 0, xp_ref[...], 0.0)
    pad_ref[HALO:HALO + SEQ_TILE, :] = xm_ref[...]
    pad_ref[HALO + SEQ_TILE:2 * HALO + SEQ_TILE, :] = jnp.where(j < last, xn_ref[...], 0.0)
    first_tap = HALO - (CONV_K - 1) // 2
    for c in range(CHUNKS_PER_STEP):
        acc = cb_ref[...]
        for k in range(CONV_K):
            start = c * CHUNK + first_tap + k
            acc = acc + pad_ref[start:start + CHUNK, :] * cw_ref[k:k + 1, :]
        u_ref[c * CHUNK:(c + 1) * CHUNK, :] = _silu(acc).astype(BF16)


def _conv(xbc, cw, cb):
    b, s, _ = xbc.shape
    steps = s // SEQ_TILE
    halo_per_tile = SEQ_TILE // HALO
    n_halo = s // HALO
    main = pl.BlockSpec((None, SEQ_TILE, CONV_DIM), lambda bi, j: (bi, j, 0))
    prev = pl.BlockSpec((None, HALO, CONV_DIM),
                        lambda bi, j: (bi, jnp.maximum(j * halo_per_tile - 1, 0), 0))
    nxt = pl.BlockSpec((None, HALO, CONV_DIM),
                       lambda bi, j: (bi, jnp.minimum((j + 1) * halo_per_tile, n_halo - 1), 0))
    resident = 4 * (SEQ_TILE + 2 * HALO) * CONV_DIM + 2 * (4 + 2) * SEQ_TILE * CONV_DIM
    return pl.pallas_call(
        _conv_kernel,
        grid=(b, steps),
        in_specs=[main, prev, nxt, _resident((SUBLANES, CONV_DIM)), _resident((1, CONV_DIM))],
        out_specs=pl.BlockSpec((None, SEQ_TILE, CONV_DIM), lambda bi, j: (bi, j, 0)),
        out_shape=jax.ShapeDtypeStruct((b, s, CONV_DIM), BF16),
        scratch_shapes=[pltpu.VMEM((SEQ_TILE + 2 * HALO, CONV_DIM), F32)],
        compiler_params=pltpu.CompilerParams(
            dimension_semantics=("parallel", "parallel"), vmem_limit_bytes=_vmem_limit(resident)),
        name="conv",
    )(xbc, xbc, xbc, cw, cb)


def _tri(lower):
    r = lax.broadcasted_iota(jnp.int32, (CHUNK, CHUNK), 0)
    c = lax.broadcasted_iota(jnp.int32, (CHUNK, CHUNK), 1)
    return jnp.where((r >= c) if lower else (r <= c), 1.0, 0.0).astype(BF16)


def _tri_matmul(tri, x):
    hi, mid, lo = _split3(x)
    out = _dot(tri, jnp.concatenate([hi, mid, lo], axis=1))
    w = x.shape[1]
    return out[:, 0:w] + out[:, w:2 * w] + out[:, 2 * w:3 * w]


def _neg_exp_row(a_log_ref):
    lane = lax.broadcasted_iota(jnp.int32, (1, DT_PAD), 1)
    return jnp.where(lane < 2 * SSM_HEADS, -jnp.exp(a_log_ref[...]), 0.0)


def _expand_heads(x, e2_ref):
    hi, lo = _split2(x)
    return _dot(jnp.concatenate([hi, lo], axis=1), e2_ref[...])


def _bwd_state_kernel(u_ref, dt_ref, bias_ref, a_ref, e2b_ref, prev_ref, st_ref):
    @pl.when(pl.program_id(1) == 0)
    def _():
        st_ref[...] = jnp.zeros_like(st_ref)

    tri_u = _tri(lower=False)
    a_row = _neg_exp_row(a_ref)
    for c in reversed(range(CHUNKS_PER_STEP)):
        rows = slice(c * CHUNK, (c + 1) * CHUNK)
        dtv = _softplus(dt_ref[rows, :] + bias_ref[...])
        rcs = _tri_matmul(tri_u, dtv * a_row)
        tot = rcs[0:1, :]
        w_exp = _expand_heads(jnp.exp(tot - rcs) * dtv, e2b_ref)
        xsw = (u_ref[rows, 0:D_SSM].astype(F32) * w_exp).astype(BF16)
        dec = _expand_heads(jnp.broadcast_to(jnp.exp(tot), (SUBLANES, LANES)), e2b_ref)[0:1, :]
        prev_ref[c] = st_ref[...].astype(BF16)
        for g in range(SSM_GROUPS):
            cols = slice(g * GROUP_W, (g + 1) * GROUP_W)
            bg = u_ref[rows, D_SSM + g * D_STATE:D_SSM + (g + 1) * D_STATE]
            st_ref[:, cols] = st_ref[:, cols] * dec[:, cols] + _dot_tn(bg, xsw[:, cols])


def _bwd_state(u, dt, bias, a_row, e2b):
    b, s, _ = u.shape
    steps = s // SEQ_TILE
    rev = lambda bi, j: (bi, steps - 1 - j, 0)
    resident = (2 * 2 * SEQ_TILE * CONV_DIM + 2 * 4 * SEQ_TILE * DT_PAD
                + 2 * 2 * SEQ_TILE * D_SSM + 4 * D_STATE * D_SSM + 2 * 2 * LANES * D_SSM)
    return pl.pallas_call(
        _bwd_state_kernel,
        grid=(b, steps),
        in_specs=[pl.BlockSpec((None, SEQ_TILE, CONV_DIM), rev),
                  pl.BlockSpec((None, SEQ_TILE, DT_PAD), rev),
                  _resident((1, DT_PAD)), _resident((1, DT_PAD)), _resident((2 * LANES, D_SSM))],
        out_specs=pl.BlockSpec((None, CHUNKS_PER_STEP, D_STATE, D_SSM),
                               lambda bi, j: (bi, steps - 1 - j, 0, 0)),
        out_shape=jax.ShapeDtypeStruct((b, s // CHUNK, D_STATE, D_SSM), BF16),
        scratch_shapes=[pltpu.VMEM((D_STATE, D_SSM), F32)],
        compiler_params=pltpu.CompilerParams(
            dimension_semantics=("arbitrary", "arbitrary"), vmem_limit_bytes=_vmem_limit(resident)),
        name="bwd_state",
    )(u, dt, bias, a_row, e2b)


def _ssd_kernel(u_ref, dt_ref, z_ref, prev_ref, bias_ref, a_ref, e2f_ref, e2b_ref, e3_ref,
                dskip_ref, nw_ref, y_ref, st_ref):
    @pl.when(pl.program_id(1) == 0)
    def _():
        st_ref[...] = jnp.zeros_like(st_ref)

    tri_both = jnp.concatenate([_tri(lower=True), _tri(lower=False)], axis=0)
    row = lax.broadcasted_iota(jnp.int32, (CHUNK, CHUNK), 0)
    col = lax.broadcasted_iota(jnp.int32, (CHUNK, CHUNK), 1)
    half = SSM_HEAD_DIM
    a_row = _neg_exp_row(a_ref)

    for c in range(CHUNKS_PER_STEP):
        rows = slice(c * CHUNK, (c + 1) * CHUNK)
        dtv = _softplus(dt_ref[rows, :] + bias_ref[...])
        both = _tri_matmul(tri_both, dtv * a_row)
        cs = both[0:CHUNK, :]
        rcs = both[CHUNK:2 * CHUNK, :]
        comb = jnp.where(col < SSM_HEADS, cs, rcs)
        comb_t = comb.T
        dt_t = dtv.T

        hi, mid, lo = _split3(comb)
        packed = jnp.where(col < 32, hi.astype(F32),
                           jnp.where(col < 64, pltpu.roll(mid.astype(F32), 32, 1),
                                     jnp.where(col < 96, pltpu.roll(lo.astype(F32), 64, 1), 0.0)))
        colb = _dot(packed.astype(BF16), e3_ref[...])

        xs_bf = u_ref[rows, 0:D_SSM]
        xs = xs_bf.astype(F32)
        bmat = [u_ref[rows, D_SSM + g * D_STATE:D_SSM + (g + 1) * D_STATE] for g in range(SSM_GROUPS)]
        cmat = [u_ref[rows, D_SSM + (SSM_GROUPS + g) * D_STATE:D_SSM + (SSM_GROUPS + g + 1) * D_STATE]
                for g in range(SSM_GROUPS)]
        cb = [_dot_nt(cmat[g], bmat[g]) for g in range(SSM_GROUPS)]

        ydiag = []
        for j in range(SSM_HEADS // 2):
            g = (2 * j) // HEADS_PER_GROUP
            ms = []
            for h in (2 * j, 2 * j + 1):
                df = colb[:, h * CHUNK:(h + 1) * CHUNK] - comb_t[h:h + 1, :]
                db = colb[:, (SSM_HEADS + h) * CHUNK:(SSM_HEADS + h + 1) * CHUNK] \
                    - comb_t[SSM_HEADS + h:SSM_HEADS + h + 1, :]
                dtf = dt_t[h:h + 1, :]
                dtb = dt_t[SSM_HEADS + h:SSM_HEADS + h + 1, :]
                dsel = jnp.where(row > col, dtf, jnp.where(row < col, dtb, dtf + dtb))
                decay = jnp.exp(jnp.where(row >= col, df, db))
                ms.append((cb[g] * decay * dsel).astype(BF16))
            xt = xs[:, j * LANES:(j + 1) * LANES]
            rhs = jnp.concatenate([jnp.where(col < half, xt, 0.0), jnp.where(col >= half, xt, 0.0)],
                                  axis=0).astype(BF16)
            ydiag.append(_dot(jnp.concatenate(ms, axis=1), rhs))

        last = cs[CHUNK - 1:CHUNK, :]
        wf = _expand_heads(jnp.exp(last - cs) * dtv, e2f_ref)
        ef = _expand_heads(jnp.exp(cs), e2f_ref)
        eb = _expand_heads(jnp.exp(rcs), e2b_ref)
        decf = _expand_heads(jnp.broadcast_to(jnp.exp(last), (SUBLANES, LANES)), e2f_ref)[0:1, :]
        xsw = (xs * wf).astype(BF16)

        for g in range(SSM_GROUPS):
            cols = slice(g * GROUP_W, (g + 1) * GROUP_W)
            state = st_ref[:, cols]
            yoff_f = _dot(cmat[g], state.astype(BF16))
            yoff_b = _dot(cmat[g], prev_ref[c, :, cols])
            tiles = range(g * GROUP_W // LANES, (g + 1) * GROUP_W // LANES)
            y = jnp.concatenate([ydiag[t] for t in tiles], axis=1)
            y = y + ef[:, cols] * yoff_f + eb[:, cols] * yoff_b + xs[:, cols] * dskip_ref[:, cols]
            y = y * _silu(z_ref[rows, cols].astype(F32))
            y = y * lax.rsqrt(jnp.mean(y * y, axis=-1, keepdims=True) + EPS) * nw_ref[:, cols]
            y_ref[rows, cols] = y.astype(BF16)
            st_ref[:, cols] = state * decf[:, cols] + _dot_tn(bmat[g], xsw[:, cols])


def _ssd(u, dt, z, prevb, bias, a_row, e2f, e2b, e3, dskip, nw):
    b, s, _ = u.shape
    steps = s // SEQ_TILE
    fwd = lambda bi, j: (bi, j, 0)
    resident = (2 * 2 * SEQ_TILE * (CONV_DIM + 2 * D_SSM) + 2 * 4 * SEQ_TILE * DT_PAD
                + 2 * 2 * SEQ_TILE * D_SSM + 4 * D_STATE * D_SSM
                + 2 * 2 * 2 * LANES * D_SSM + 2 * LANES * 2 * SSM_HEADS * CHUNK
                + 4 * CHUNK * 2 * SSM_HEADS * CHUNK)
    return pl.pallas_call(
        _ssd_kernel,
        grid=(b, steps),
        in_specs=[pl.BlockSpec((None, SEQ_TILE, CONV_DIM), fwd),
                  pl.BlockSpec((None, SEQ_TILE, DT_PAD), fwd),
                  pl.BlockSpec((None, SEQ_TILE, D_SSM), fwd),
                  pl.BlockSpec((None, CHUNKS_PER_STEP, D_STATE, D_SSM), lambda bi, j: (bi, j, 0, 0)),
                  _resident((1, DT_PAD)), _resident((1, DT_PAD)),
                  _resident((2 * LANES, D_SSM)), _resident((2 * LANES, D_SSM)),
                  _resident((LANES, 2 * SSM_HEADS * CHUNK)),
                  _resident((1, D_SSM)), _resident((1, D_SSM))],
        out_specs=pl.BlockSpec((None, SEQ_TILE, D_SSM), fwd),
        out_shape=jax.ShapeDtypeStruct((b, s, D_SSM), BF16),
        scratch_shapes=[pltpu.VMEM((D_STATE, D_SSM), F32)],
        compiler_params=pltpu.CompilerParams(
            dimension_semantics=("arbitrary", "arbitrary"), vmem_limit_bytes=_vmem_limit(resident)),
        name="ssd",
    )(u, dt, z, prevb, bias, a_row, e2f, e2b, e3, dskip, nw)


def _attn_kernel(slope_ref, sink_ref, q_ref, kp_ref, kc_ref, kn_ref, vp_ref, vc_ref, vn_ref, o_ref):
    i = pl.program_id(1)
    last = pl.num_programs(1) - 1
    keys = 3 * CHUNK
    half = HEAD_DIM
    srow = lax.broadcasted_iota(jnp.int32, (keys, CHUNK), 0)
    tcol = lax.broadcasted_iota(jnp.int32, (keys, CHUNK), 1)
    idist = jnp.abs(tcol + CHUNK - srow)
    dist = idist.astype(F32)
    lo_key = jnp.where(i == 0, CHUNK, 0)
    hi_key = jnp.where(i == last, 2 * CHUNK, keys)
    valid = (idist <= WINDOW) & (srow >= lo_key) & (srow < hi_key)
    klane = lax.broadcasted_iota(jnp.int32, (keys, LANES), 1)
    vrow = lax.broadcasted_iota(jnp.int32, (LANES, keys), 0)
    orow = lax.broadcasted_iota(jnp.int32, (LANES, CHUNK), 0)

    for m in range(D_KV // LANES):
        lanes = slice(m * LANES, (m + 1) * LANES)
        kt = jnp.concatenate([kp_ref[:, lanes], kc_ref[:, lanes], kn_ref[:, lanes]], axis=0)
        vt = jnp.concatenate([vp_ref[:, lanes], vc_ref[:, lanes], vn_ref[:, lanes]], axis=0)
        kf = kt.astype(F32)
        kk = jnp.concatenate([jnp.where(klane < half, kf, 0.0), jnp.where(klane >= half, kf, 0.0)],
                             axis=0).astype(BF16)
        vt_t = vt.astype(F32).T
        vv = jnp.concatenate([jnp.where(vrow < half, vt_t, 0.0), jnp.where(vrow >= half, vt_t, 0.0)],
                             axis=1).astype(BF16)
        tiles_per_kv_tile = (D_ATTN // LANES) // (D_KV // LANES)
        for jj in range(tiles_per_kv_tile):
            j = m * tiles_per_kv_tile + jj
            st = _dot_nt(kk, q_ref[:, j * LANES:(j + 1) * LANES])
            ps, inv = [], []
            for e in range(2):
                slot = 2 * j + e
                s = st[e * keys:(e + 1) * keys, :] - slope_ref[slot] * dist
                s = jnp.where(valid, s, -jnp.inf)
                mx = jnp.maximum(jnp.max(s, axis=0, keepdims=True), sink_ref[slot])
                p = jnp.exp(s - mx)
                den = jnp.sum(p, axis=0, keepdims=True) + jnp.exp(sink_ref[slot] - mx)
                ps.append(p.astype(BF16))
                inv.append(1.0 / den)
            ot = _dot(vv, jnp.concatenate(ps, axis=0))
            ot = ot * jnp.where(orow < half, inv[0], inv[1])
            o_ref[:, j * LANES:(j + 1) * LANES] = ot.T.astype(BF16)


def _attn(q, k, v, slopes, sink):
    b, s, _ = q.shape
    nblk = s // CHUNK
    cur = lambda bi, i: (bi, i, 0)
    prv = lambda bi, i: (bi, jnp.maximum(i - 1, 0), 0)
    nxt = lambda bi, i: (bi, jnp.minimum(i + 1, nblk - 1), 0)
    smem = pl.BlockSpec(memory_space=pltpu.SMEM)
    kv = lambda f: pl.BlockSpec((None, CHUNK, D_KV), f)
    resident = 2 * 2 * CHUNK * (2 * D_ATTN + 6 * D_KV) + 16 * 4 * 3 * CHUNK * CHUNK
    return pl.pallas_call(
        _attn_kernel,
        grid=(b, nblk),
        in_specs=[smem, smem, pl.BlockSpec((None, CHUNK, D_ATTN), cur),
                  kv(prv), kv(cur), kv(nxt), kv(prv), kv(cur), kv(nxt)],
        out_specs=pl.BlockSpec((None, CHUNK, D_ATTN), cur),
        out_shape=jax.ShapeDtypeStruct((b, s, D_ATTN), BF16),
        compiler_params=pltpu.CompilerParams(
            dimension_semantics=("parallel", "parallel"), vmem_limit_bytes=_vmem_limit(resident)),
        name="attn",
    )(slopes, sink, q, k, k, k, v, v, v)


def _out_ffn2_kernel(x_ref, ys_ref, ya_ref, wos_ref, woa_ref, nw2_ref, wg_ref, wu_ref, wd_ref,
                     nwf_ref, o_ref):
    x2 = x_ref[...] + _dot(ys_ref[...], wos_ref[...]) + _dot(ya_ref[...], woa_ref[...])
    xn = _rmsnorm(x2, nw2_ref[...]).astype(BF16)
    x3 = x2 + 0.5 * _swiglu(xn, wg_ref, wu_ref, wd_ref)
    o_ref[...] = _rmsnorm(x3, nwf_ref[...])


def _out_ffn2(x1, ys, ya, wos, woa, nw2, wg, wu, wd, nwf):
    n = x1.shape[0]

    def tile(w):
        return pl.BlockSpec((TOK_TILE, w), lambda i: (i, 0))

    resident = (2 * (3 * D_MODEL * D_FF + (D_SSM + D_ATTN) * D_MODEL)
                + 4 * 4 * TOK_TILE * D_MODEL + 2 * 2 * TOK_TILE * (D_SSM + D_ATTN)
                + 3 * 4 * TOK_TILE * FF_CHUNK)
    return pl.pallas_call(
        _out_ffn2_kernel,
        grid=(n // TOK_TILE,),
        in_specs=[tile(D_MODEL), tile(D_SSM), tile(D_ATTN),
                  _resident((D_SSM, D_MODEL)), _resident((D_ATTN, D_MODEL)), _resident((1, D_MODEL)),
                  _resident((D_MODEL, D_FF)), _resident((D_MODEL, D_FF)), _resident((D_FF, D_MODEL)),
                  _resident((1, D_MODEL))],
        out_specs=tile(D_MODEL),
        out_shape=jax.ShapeDtypeStruct((n, D_MODEL), F32),
        compiler_params=pltpu.CompilerParams(
            dimension_semantics=("parallel",), vmem_limit_bytes=_vmem_limit(resident)),
        name="out_ffn2",
    )(x1, ys, ya, wos, woa, nw2, wg, wu, wd, nwf)


def _head_selection(offset):
    e = np.zeros((2 * LANES, D_SSM), np.float32)
    for h in range(SSM_HEADS):
        e[offset + h, h * SSM_HEAD_DIM:(h + 1) * SSM_HEAD_DIM] = 1.0
        e[LANES + offset + h, h * SSM_HEAD_DIM:(h + 1) * SSM_HEAD_DIM] = 1.0
    return jnp.asarray(e, BF16)


def _column_broadcast_selection():
    e = np.zeros((LANES, 2 * SSM_HEADS * CHUNK), np.float32)
    for t in range(3):
        for i in range(2 * SSM_HEADS):
            e[32 * t + i, i * CHUNK:(i + 1) * CHUNK] = 1.0
    return jnp.asarray(e, BF16)


def _prepare(norm_ffn1_w, ffn1_w_gate, ffn1_w_up, ffn1_w_down, norm_mix_w, w_in, conv_w, conv_b,
             dt_bias_fwd, dt_bias_bwd, a_log_fwd, a_log_bwd, d_skip, ssm_norm_w, attn_sink, w_out,
             norm_ffn2_w, ffn2_w_gate, ffn2_w_up, ffn2_w_down, norm_final_w):
    row = lambda v: v.reshape(1, -1).astype(F32)
    splits = np.cumsum((D_SSM, CONV_DIM, SSM_HEADS, SSM_HEADS, D_ATTN, D_KV))
    wz, wx, wdtf, wdtb, wq, wk, wv = jnp.split(w_in, splits, axis=1)
    wdt = jnp.concatenate([wdtf, wdtb, jnp.zeros((D_MODEL, DT_PAD - 2 * SSM_HEADS), F32)], axis=1)
    order = np.asarray(ATTN_HEAD_ORDER)
    cols = (order[:, None] * HEAD_DIM + np.arange(HEAD_DIM)[None, :]).reshape(-1)
    wq = wq[:, cols]
    pad_dt = jnp.zeros((DT_PAD - 2 * SSM_HEADS,), F32)
    slopes = jnp.exp2(-(8.0 / N_HEADS) * jnp.arange(1, N_HEADS + 1, dtype=F32))
    return dict(
        nw1=row(norm_ffn1_w), wg1=ffn1_w_gate.astype(BF16), wu1=ffn1_w_up.astype(BF16),
        wd1=ffn1_w_down.astype(BF16),
        nwm=row(norm_mix_w), wz=wz.astype(BF16), wx=wx.astype(BF16), wdt=wdt.astype(BF16),
        wq=wq.astype(BF16), wk=wk.astype(BF16), wv=wv.astype(BF16),
        cw=jnp.concatenate([conv_w, jnp.zeros((SUBLANES - CONV_K, CONV_DIM), F32)], axis=0),
        cb=row(conv_b),
        dt_bias=row(jnp.concatenate([dt_bias_fwd, dt_bias_bwd, pad_dt])),
        a_log=row(jnp.concatenate([a_log_fwd, a_log_bwd, pad_dt])),
        e2f=_head_selection(0), e2b=_head_selection(SSM_HEADS), e3=_column_broadcast_selection(),
        dskip=row(jnp.repeat(d_skip, SSM_HEAD_DIM)), nws=row(ssm_norm_w),
        slopes=slopes[order], sink=attn_sink.astype(F32)[order],
        wos=w_out[:D_SSM].astype(BF16), woa=w_out[D_SSM:][cols].astype(BF16),
        nw2=row(norm_ffn2_w), wg2=ffn2_w_gate.astype(BF16), wu2=ffn2_w_up.astype(BF16),
        wd2=ffn2_w_down.astype(BF16), nwf=row(norm_final_w),
    )


def _trunk(x, p):
    b, s, _ = x.shape
    assert s % SEQ_TILE == 0 and (b * s) % TOK_TILE == 0 and x.shape[2] == D_MODEL
    n = b * s
    x1 = _ffn1(x.reshape(n, D_MODEL), p["nw1"], p["wg1"], p["wu1"], p["wd1"])
    z, xbc, dt, q, k, v = _inproj(x1, p["nwm"], p["wz"], p["wx"], p["wdt"], p["wq"], p["wk"], p["wv"])
    seq = lambda t: t.reshape(b, s, t.shape[-1])
    u = _conv(seq(xbc), p["cw"], p["cb"])
    dt = seq(dt)
    prevb = _bwd_state(u, dt, p["dt_bias"], p["a_log"], p["e2b"])
    y_ssm = _ssd(u, dt, seq(z), prevb, p["dt_bias"], p["a_log"], p["e2f"], p["e2b"], p["e3"],
                 p["dskip"], p["nws"])
    y_att = _attn(seq(q), seq(k), seq(v), p["slopes"], p["sink"])
    out = _out_ffn2(x1, y_ssm.reshape(n, D_SSM), y_att.reshape(n, D_ATTN), p["wos"], p["woa"],
                    p["nw2"], p["wg2"], p["wu2"], p["wd2"], p["nwf"])
    return out.reshape(b, s, D_MODEL)


def kernel(x_prompt, x_sample, norm_ffn1_w, ffn1_w_gate, ffn1_w_up, ffn1_w_down, norm_mix_w, w_in, conv_w, conv_b, dt_bias_fwd, dt_bias_bwd, a_log_fwd, a_log_bwd, d_skip, ssm_norm_w, attn_sink, w_out, norm_ffn2_w, ffn2_w_gate, ffn2_w_up, ffn2_w_down, norm_final_w):
    p = _prepare(norm_ffn1_w[0], ffn1_w_gate[0], ffn1_w_up[0], ffn1_w_down[0], norm_mix_w[0], w_in[0],
                 conv_w[0], conv_b[0], dt_bias_fwd[0], dt_bias_bwd[0], a_log_fwd[0], a_log_bwd[0],
                 d_skip[0], ssm_norm_w[0], attn_sink[0], w_out[0], norm_ffn2_w[0], ffn2_w_gate[0],
                 ffn2_w_up[0], ffn2_w_down[0], norm_final_w)
    return (_trunk(x_prompt, p), _trunk(x_sample, p))
```

```python
import functools
import math

import jax
import jax.numpy as jnp
import numpy as np
from jax import lax
from jax.experimental import pallas as pl
from jax.experimental.pallas import tpu as pltpu

F32 = jnp.float32
BF16 = jnp.bfloat16

D_MODEL = 1024
D_FF = 2816
SSM_HEADS = 16
SSM_HEAD_DIM = 64
D_SSM = SSM_HEADS * SSM_HEAD_DIM
SSM_GROUPS = 2
D_STATE = 128
CONV_K = 5
CONV_DIM = D_SSM + 2 * SSM_GROUPS * D_STATE
N_HEADS = 16
KV_HEADS = 4
HEAD_DIM = 64
D_ATTN = N_HEADS * HEAD_DIM
D_KV = KV_HEADS * HEAD_DIM
WINDOW = 128
EPS = 1e-6

LANES = 128
SUBLANES = 8
VMEM_BYTES_V7X = 64 * 1024 * 1024

CHUNK = LANES
TOK_TILE = 512
SEQ_TILE = 512
CHUNKS_PER_STEP = SEQ_TILE // CHUNK
MXU_TILE = 256
FF_SPLIT = (D_FF // MXU_TILE + 1) // 2 * MXU_TILE
FF_CHUNK = FF_SPLIT
HALO = SUBLANES
CONV_PIECE = 2 * MXU_TILE
DT_PAD = LANES
HEADS_PER_GROUP = SSM_HEADS // SSM_GROUPS
GROUP_W = HEADS_PER_GROUP * SSM_HEAD_DIM

ATTN_HEAD_ORDER = (0, 4, 1, 5, 2, 6, 3, 7, 8, 12, 9, 13, 10, 14, 11, 15)


def _vmem_limit(resident_bytes):
    return int(min(VMEM_BYTES_V7X - 8 * 1024 * 1024, 2 * resident_bytes + 8 * 1024 * 1024))


def _resident(shape):
    nd = len(shape)
    return pl.BlockSpec(shape, lambda *_: (0,) * nd, pipeline_mode=pl.Buffered(1))


def _dot(a, b):
    return jnp.dot(a, b, preferred_element_type=F32)


def _dot_nt(a, b):
    return lax.dot_general(a, b, (((1,), (1,)), ((), ())), preferred_element_type=F32)


def _dot_tn(a, b):
    return lax.dot_general(a, b, (((0,), (0,)), ((), ())), preferred_element_type=F32)


def _rmsnorm(x, w):
    return x * lax.rsqrt(jnp.mean(x * x, axis=-1, keepdims=True) + EPS) * w


def _silu(x):
    return x * jax.nn.sigmoid(x)


def _softplus(x):
    return jnp.maximum(x, 0.0) + jnp.log1p(jnp.exp(-jnp.abs(x)))


def _split2(x):
    hi = x.astype(BF16)
    lo = (x - hi.astype(F32)).astype(BF16)
    return hi, lo


def _split3(x):
    hi = x.astype(BF16)
    r = x - hi.astype(F32)
    mid = r.astype(BF16)
    lo = (r - mid.astype(F32)).astype(BF16)
    return hi, mid, lo


def _swiglu(xn, wg_ref, wu_ref, wd_ref):
    acc = None
    for cols in (slice(0, FF_SPLIT), slice(FF_SPLIT, D_FF)):
        g = _dot(xn, wg_ref[:, cols])
        u = _dot(xn, wu_ref[:, cols])
        part = _dot((_silu(g) * u).astype(BF16), wd_ref[cols, :])
        acc = part if acc is None else acc + part
    return acc


def _ffn1_kernel(x_ref, nw_ref, wg_ref, wu_ref, wd_ref, o_ref):
    x = x_ref[...]
    xn = _rmsnorm(x, nw_ref[...]).astype(BF16)
    o_ref[...] = x + 0.5 * _swiglu(xn, wg_ref, wu_ref, wd_ref)


def _ffn1(x2d, nw, wg, wu, wd):
    n = x2d.shape[0]
    tile = pl.BlockSpec((TOK_TILE, D_MODEL), lambda i: (i, 0))
    resident = 2 * 3 * D_MODEL * D_FF + 4 * 4 * TOK_TILE * D_MODEL + 3 * 4 * TOK_TILE * FF_CHUNK
    return pl.pallas_call(
        _ffn1_kernel,
        grid=(n // TOK_TILE,),
        in_specs=[tile, _resident((1, D_MODEL)), _resident((D_MODEL, D_FF)),
                  _resident((D_MODEL, D_FF)), _resident((D_FF, D_MODEL))],
        out_specs=tile,
        out_shape=jax.ShapeDtypeStruct((n, D_MODEL), F32),
        compiler_params=pltpu.CompilerParams(
            dimension_semantics=("parallel",), vmem_limit_bytes=_vmem_limit(resident)),
        name="ffn1",
    )(x2d, nw, wg, wu, wd)


def _tri(lower):
    r = lax.broadcasted_iota(jnp.int32, (CHUNK, CHUNK), 0)
    c = lax.broadcasted_iota(jnp.int32, (CHUNK, CHUNK), 1)
    return jnp.where((r >= c) if lower else (r <= c), 1.0, 0.0).astype(BF16)


def _tri_matmul(tri, x):
    hi, mid, lo = _split3(x)
    out = _dot(tri, jnp.concatenate([hi, mid, lo], axis=1))
    w = x.shape[1]
    return out[:, 0:w] + out[:, w:2 * w] + out[:, 2 * w:3 * w]


def _neg_exp_row(a_log_ref):
    lane = lax.broadcasted_iota(jnp.int32, (1, DT_PAD), 1)
    return jnp.where(lane < 2 * SSM_HEADS, -jnp.exp(a_log_ref[...]), 0.0)


def _expand_heads(x, e2_ref):
    hi, lo = _split2(x)
    return _dot(jnp.concatenate([hi, lo], axis=1), e2_ref[...])


def _mix_in_kernel(xm_ref, xp_ref, xn_ref, nw_ref, wz_ref, wx_ref, wdt_ref, wq_ref, wk_ref, wv_ref,
                   cw_ref, cb_ref, bias_ref, a_ref, e2b_ref,
                   z_ref, u_ref, dt_ref, q_ref, k_ref, v_ref, prev_ref, pad_ref, st_ref):
    j = pl.program_id(1)
    last = pl.num_programs(1) - 1

    @pl.when(j == 0)
    def _():
        st_ref[...] = jnp.zeros_like(st_ref)

    h = _rmsnorm(xm_ref[...], nw_ref[...]).astype(BF16)
    h_halo = _rmsnorm(jnp.concatenate([xp_ref[...], xn_ref[...]], axis=0), nw_ref[...]).astype(BF16)
    dt = _dot(h, wdt_ref[...])
    dt_ref[...] = dt

    def side_projection(piece):
        if piece == 0:
            z_ref[...] = _dot(h, wz_ref[...]).astype(BF16)
        elif piece == 1:
            q_ref[...] = (_dot(h, wq_ref[...]) * (1.0 / math.sqrt(HEAD_DIM))).astype(BF16)
        else:
            k_ref[...] = _dot(h, wk_ref[...]).astype(BF16)
            v_ref[...] = _dot(h, wv_ref[...]).astype(BF16)

    first_tap = HALO - (CONV_K - 1) // 2
    for piece in range(CONV_DIM // CONV_PIECE):
        cols = slice(piece * CONV_PIECE, (piece + 1) * CONV_PIECE)
        xbc_halo = _dot(h_halo, wx_ref[:, cols])
        pad_ref[0:HALO, cols] = jnp.where(j < last, xbc_halo[0:HALO, :], 0.0)
        pad_ref[HALO:HALO + SEQ_TILE, cols] = _dot(h, wx_ref[:, cols])
        pad_ref[HALO + SEQ_TILE:2 * HALO + SEQ_TILE, cols] = jnp.where(j > 0, xbc_halo[HALO:2 * HALO, :], 0.0)
        for c in range(CHUNKS_PER_STEP):
            acc = cb_ref[:, cols]
            for k in range(CONV_K):
                start = c * CHUNK + first_tap + k
                acc = acc + pad_ref[start:start + CHUNK, cols] * cw_ref[k:k + 1, cols]
            u_ref[c * CHUNK:(c + 1) * CHUNK, cols] = _silu(acc).astype(BF16)
        side_projection(piece)

    tri_u = _tri(lower=False)
    a_row = _neg_exp_row(a_ref)
    for c in reversed(range(CHUNKS_PER_STEP)):
        rows = slice(c * CHUNK, (c + 1) * CHUNK)
        u = u_ref[rows, :]
        dtv = _softplus(dt[rows, :] + bias_ref[...])
        rcs = _tri_matmul(tri_u, dtv * a_row)
        tot = rcs[0:1, :]
        w_exp = _expand_heads(jnp.exp(tot - rcs) * dtv, e2b_ref)
        xsw = (u[:, 0:D_SSM].astype(F32) * w_exp).astype(BF16)
        dec = _expand_heads(jnp.broadcast_to(jnp.exp(tot), (SUBLANES, LANES)), e2b_ref)[0:1, :]
        prev_ref[c] = st_ref[...].astype(BF16)
        for g in range(SSM_GROUPS):
            cols = slice(g * GROUP_W, (g + 1) * GROUP_W)
            bg = u[:, D_SSM + g * D_STATE:D_SSM + (g + 1) * D_STATE]
            st_ref[:, cols] = st_ref[:, cols] * dec[:, cols] + _dot_tn(bg, xsw[:, cols])


def _mix_in(x1, nw, wz, wx, wdt, wq, wk, wv, cw, cb, bias, a_log, e2b):
    b, s, _ = x1.shape
    steps = s // SEQ_TILE
    halo_per_tile = SEQ_TILE // HALO
    n_halo = s // HALO
    widths = (D_SSM, CONV_DIM, DT_PAD, D_ATTN, D_KV, D_KV)
    dtypes = (BF16, BF16, F32, BF16, BF16, BF16)
    rev = lambda bi, j: (bi, steps - 1 - j, 0)
    prv = lambda bi, j: (bi, jnp.maximum((steps - 1 - j) * halo_per_tile - 1, 0), 0)
    nxt = lambda bi, j: (bi, jnp.minimum((steps - j) * halo_per_tile, n_halo - 1), 0)
    resident = (2 * D_MODEL * sum(widths) + 2 * 2 * LANES * D_SSM
                + 2 * 4 * SEQ_TILE * (D_MODEL + sum(widths)) + 2 * 2 * SEQ_TILE * D_SSM
                + 4 * (SEQ_TILE + 2 * HALO) * CONV_DIM + 4 * D_STATE * D_SSM)
    return pl.pallas_call(
        _mix_in_kernel,
        grid=(b, steps),
        in_specs=[pl.BlockSpec((None, SEQ_TILE, D_MODEL), rev),
                  pl.BlockSpec((None, HALO, D_MODEL), prv),
                  pl.BlockSpec((None, HALO, D_MODEL), nxt),
                  _resident((1, D_MODEL))] + [_resident((D_MODEL, w)) for w in widths]
                 + [_resident((SUBLANES, CONV_DIM)), _resident((1, CONV_DIM)),
                    _resident((1, DT_PAD)), _resident((1, DT_PAD)), _resident((2 * LANES, D_SSM))],
        out_specs=[pl.BlockSpec((None, SEQ_TILE, w), rev) for w in widths]
                  + [pl.BlockSpec((None, CHUNKS_PER_STEP, D_STATE, D_SSM),
                                  lambda bi, j: (bi, steps - 1 - j, 0, 0))],
        out_shape=[jax.ShapeDtypeStruct((b, s, w), d) for w, d in zip(widths, dtypes)]
                  + [jax.ShapeDtypeStruct((b, s // CHUNK, D_STATE, D_SSM), BF16)],
        scratch_shapes=[pltpu.VMEM((SEQ_TILE + 2 * HALO, CONV_DIM), F32),
                        pltpu.VMEM((D_STATE, D_SSM), F32)],
        compiler_params=pltpu.CompilerParams(
            dimension_semantics=("arbitrary", "arbitrary"), vmem_limit_bytes=_vmem_limit(resident)),
        name="mix_in",
    )(x1, x1, x1, nw, wz, wx, wdt, wq, wk, wv, cw, cb, bias, a_log, e2b)


def _ssd_kernel(u_ref, dt_ref, z_ref, prev_ref, bias_ref, a_ref, e2f_ref, e2b_ref, e3_ref,
                dskip_ref, nw_ref, y_ref, st_ref):
    @pl.when(pl.program_id(1) == 0)
    def _():
        st_ref[...] = jnp.zeros_like(st_ref)

    tri_both = jnp.concatenate([_tri(lower=True), _tri(lower=False)], axis=0)
    row = lax.broadcasted_iota(jnp.int32, (CHUNK, CHUNK), 0)
    col = lax.broadcasted_iota(jnp.int32, (CHUNK, CHUNK), 1)
    half = SSM_HEAD_DIM
    a_row = _neg_exp_row(a_ref)

    for c in range(CHUNKS_PER_STEP):
        rows = slice(c * CHUNK, (c + 1) * CHUNK)
        dtv = _softplus(dt_ref[rows, :] + bias_ref[...])
        both = _tri_matmul(tri_both, dtv * a_row)
        cs = both[0:CHUNK, :]
        rcs = both[CHUNK:2 * CHUNK, :]
        comb = jnp.where(col < SSM_HEADS, cs, rcs)
        comb_t = comb.T
        dt_t = dtv.T

        hi, mid, lo = _split3(comb)
        packed = jnp.where(col < 32, hi.astype(F32),
                           jnp.where(col < 64, pltpu.roll(mid.astype(F32), 32, 1),
                                     jnp.where(col < 96, pltpu.roll(lo.astype(F32), 64, 1), 0.0)))
        colb = _dot(packed.astype(BF16), e3_ref[...])

        xs_bf = u_ref[rows, 0:D_SSM]
        xs = xs_bf.astype(F32)
        bmat = [u_ref[rows, D_SSM + g * D_STATE:D_SSM + (g + 1) * D_STATE] for g in range(SSM_GROUPS)]
        cmat = [u_ref[rows, D_SSM + (SSM_GROUPS + g) * D_STATE:D_SSM + (SSM_GROUPS + g + 1) * D_STATE]
                for g in range(SSM_GROUPS)]
        cb = [_dot_nt(cmat[g], bmat[g]) for g in range(SSM_GROUPS)]

        ydiag = []
        for j in range(SSM_HEADS // 2):
            g = (2 * j) // HEADS_PER_GROUP
            ms = []
            for h in (2 * j, 2 * j + 1):
                df = colb[:, h * CHUNK:(h + 1) * CHUNK] - comb_t[h:h + 1, :]
                db = colb[:, (SSM_HEADS + h) * CHUNK:(SSM_HEADS + h + 1) * CHUNK] \
                    - comb_t[SSM_HEADS + h:SSM_HEADS + h + 1, :]
                dtf = dt_t[h:h + 1, :]
                dtb = dt_t[SSM_HEADS + h:SSM_HEADS + h + 1, :]
                dsel = jnp.where(row > col, dtf, jnp.where(row < col, dtb, dtf + dtb))
                decay = jnp.exp(jnp.where(row >= col, df, db))
                ms.append((cb[g] * decay * dsel).astype(BF16))
            xt = xs[:, j * LANES:(j + 1) * LANES]
            rhs = jnp.concatenate([jnp.where(col < half, xt, 0.0), jnp.where(col >= half, xt, 0.0)],
                                  axis=0).astype(BF16)
            ydiag.append(_dot(jnp.concatenate(ms, axis=1), rhs))

        last = cs[CHUNK - 1:CHUNK, :]
        wf = _expand_heads(jnp.exp(last - cs) * dtv, e2f_ref)
        ef = _expand_heads(jnp.exp(cs), e2f_ref)
        eb = _expand_heads(jnp.exp(rcs), e2b_ref)
        decf = _expand_heads(jnp.broadcast_to(jnp.exp(last), (SUBLANES, LANES)), e2f_ref)[0:1, :]
        xsw = (xs * wf).astype(BF16)

        for g in range(SSM_GROUPS):
            cols = slice(g * GROUP_W, (g + 1) * GROUP_W)
            state = st_ref[:, cols]
            yoff_f = _dot(cmat[g], state.astype(BF16))
            yoff_b = _dot(cmat[g], prev_ref[c, :, cols])
            tiles = range(g * GROUP_W // LANES, (g + 1) * GROUP_W // LANES)
            y = jnp.concatenate([ydiag[t] for t in tiles], axis=1)
            y = y + ef[:, cols] * yoff_f + eb[:, cols] * yoff_b + xs[:, cols] * dskip_ref[:, cols]
            y = y * _silu(z_ref[rows, cols].astype(F32))
            y = y * lax.rsqrt(jnp.mean(y * y, axis=-1, keepdims=True) + EPS) * nw_ref[:, cols]
            y_ref[rows, cols] = y.astype(BF16)
            st_ref[:, cols] = state * decf[:, cols] + _dot_tn(bmat[g], xsw[:, cols])


def _ssd(u, dt, z, prevb, bias, a_row, e2f, e2b, e3, dskip, nw):
    b, s, _ = u.shape
    steps = s // SEQ_TILE
    fwd = lambda bi, j: (bi, j, 0)
    resident = (2 * 2 * SEQ_TILE * (CONV_DIM + 2 * D_SSM) + 2 * 4 * SEQ_TILE * DT_PAD
                + 2 * 2 * SEQ_TILE * D_SSM + 4 * D_STATE * D_SSM
                + 2 * 2 * 2 * LANES * D_SSM + 2 * LANES * 2 * SSM_HEADS * CHUNK
                + 4 * CHUNK * 2 * SSM_HEADS * CHUNK)
    return pl.pallas_call(
        _ssd_kernel,
        grid=(b, steps),
        in_specs=[pl.BlockSpec((None, SEQ_TILE, CONV_DIM), fwd),
                  pl.BlockSpec((None, SEQ_TILE, DT_PAD), fwd),
                  pl.BlockSpec((None, SEQ_TILE, D_SSM), fwd),
                  pl.BlockSpec((None, CHUNKS_PER_STEP, D_STATE, D_SSM), lambda bi, j: (bi, j, 0, 0)),
                  _resident((1, DT_PAD)), _resident((1, DT_PAD)),
                  _resident((2 * LANES, D_SSM)), _resident((2 * LANES, D_SSM)),
                  _resident((LANES, 2 * SSM_HEADS * CHUNK)),
                  _resident((1, D_SSM)), _resident((1, D_SSM))],
        out_specs=pl.BlockSpec((None, SEQ_TILE, D_SSM), fwd),
        out_shape=jax.ShapeDtypeStruct((b, s, D_SSM), BF16),
        scratch_shapes=[pltpu.VMEM((D_STATE, D_SSM), F32)],
        compiler_params=pltpu.CompilerParams(
            dimension_semantics=("arbitrary", "arbitrary"), vmem_limit_bytes=_vmem_limit(resident)),
        name="ssd",
    )(u, dt, z, prevb, bias, a_row, e2f, e2b, e3, dskip, nw)


def _attn_kernel(slope_ref, sink_ref, q_ref, kp_ref, kc_ref, kn_ref, vp_ref, vc_ref, vn_ref, o_ref):
    i = pl.program_id(1)
    last = pl.num_programs(1) - 1
    keys = 3 * CHUNK
    half = HEAD_DIM
    srow = lax.broadcasted_iota(jnp.int32, (keys, CHUNK), 0)
    tcol = lax.broadcasted_iota(jnp.int32, (keys, CHUNK), 1)
    idist = jnp.abs(tcol + CHUNK - srow)
    dist = idist.astype(F32)
    lo_key = jnp.where(i == 0, CHUNK, 0)
    hi_key = jnp.where(i == last, 2 * CHUNK, keys)
    valid = (idist <= WINDOW) & (srow >= lo_key) & (srow < hi_key)
    klane = lax.broadcasted_iota(jnp.int32, (keys, LANES), 1)
    vrow = lax.broadcasted_iota(jnp.int32, (LANES, keys), 0)
    orow = lax.broadcasted_iota(jnp.int32, (LANES, CHUNK), 0)

    for m in range(D_KV // LANES):
        lanes = slice(m * LANES, (m + 1) * LANES)
        kt = jnp.concatenate([kp_ref[:, lanes], kc_ref[:, lanes], kn_ref[:, lanes]], axis=0)
        vt = jnp.concatenate([vp_ref[:, lanes], vc_ref[:, lanes], vn_ref[:, lanes]], axis=0)
        kf = kt.astype(F32)
        kk = jnp.concatenate([jnp.where(klane < half, kf, 0.0), jnp.where(klane >= half, kf, 0.0)],
                             axis=0).astype(BF16)
        vt_t = vt.astype(F32).T
        vv = jnp.concatenate([jnp.where(vrow < half, vt_t, 0.0), jnp.where(vrow >= half, vt_t, 0.0)],
                             axis=1).astype(BF16)
        tiles_per_kv_tile = (D_ATTN // LANES) // (D_KV // LANES)
        for jj in range(tiles_per_kv_tile):
            j = m * tiles_per_kv_tile + jj
            st = _dot_nt(kk, q_ref[:, j * LANES:(j + 1) * LANES])
            ps, inv = [], []
            for e in range(2):
                slot = 2 * j + e
                s = st[e * keys:(e + 1) * keys, :] - slope_ref[slot] * dist
                s = jnp.where(valid, s, -jnp.inf)
                mx = jnp.maximum(jnp.max(s, axis=0, keepdims=True), sink_ref[slot])
                p = jnp.exp(s - mx)
                den = jnp.sum(p, axis=0, keepdims=True) + jnp.exp(sink_ref[slot] - mx)
                ps.append(p.astype(BF16))
                inv.append(1.0 / den)
            ot = _dot(vv, jnp.concatenate(ps, axis=0))
            ot = ot * jnp.where(orow < half, inv[0], inv[1])
            o_ref[:, j * LANES:(j + 1) * LANES] = ot.T.astype(BF16)


def _attn(q, k, v, slopes, sink):
    b, s, _ = q.shape
    nblk = s // CHUNK
    cur = lambda bi, i: (bi, i, 0)
    prv = lambda bi, i: (bi, jnp.maximum(i - 1, 0), 0)
    nxt = lambda bi, i: (bi, jnp.minimum(i + 1, nblk - 1), 0)
    smem = pl.BlockSpec(memory_space=pltpu.SMEM)
    kv = lambda f: pl.BlockSpec((None, CHUNK, D_KV), f)
    resident = 2 * 2 * CHUNK * (2 * D_ATTN + 6 * D_KV) + 16 * 4 * 3 * CHUNK * CHUNK
    return pl.pallas_call(
        _attn_kernel,
        grid=(b, nblk),
        in_specs=[smem, smem, pl.BlockSpec((None, CHUNK, D_ATTN), cur),
                  kv(prv), kv(cur), kv(nxt), kv(prv), kv(cur), kv(nxt)],
        out_specs=pl.BlockSpec((None, CHUNK, D_ATTN), cur),
        out_shape=jax.ShapeDtypeStruct((b, s, D_ATTN), BF16),
        compiler_params=pltpu.CompilerParams(
            dimension_semantics=("parallel", "parallel"), vmem_limit_bytes=_vmem_limit(resident)),
        name="attn",
    )(slopes, sink, q, k, k, k, v, v, v)


def _out_ffn2_kernel(x_ref, ys_ref, ya_ref, wos_ref, woa_ref, nw2_ref, wg_ref, wu_ref, wd_ref,
                     nwf_ref, o_ref):
    x2 = x_ref[...] + _dot(ys_ref[...], wos_ref[...]) + _dot(ya_ref[...], woa_ref[...])
    xn = _rmsnorm(x2, nw2_ref[...]).astype(BF16)
    x3 = x2 + 0.5 * _swiglu(xn, wg_ref, wu_ref, wd_ref)
    o_ref[...] = _rmsnorm(x3, nwf_ref[...])


def _out_ffn2(x1, ys, ya, wos, woa, nw2, wg, wu, wd, nwf):
    n = x1.shape[0]

    def tile(w):
        return pl.BlockSpec((TOK_TILE, w), lambda i: (i, 0))

    resident = (2 * (3 * D_MODEL * D_FF + (D_SSM + D_ATTN) * D_MODEL)
                + 4 * 4 * TOK_TILE * D_MODEL + 2 * 2 * TOK_TILE * (D_SSM + D_ATTN)
                + 3 * 4 * TOK_TILE * FF_CHUNK)
    return pl.pallas_call(
        _out_ffn2_kernel,
        grid=(n // TOK_TILE,),
        in_specs=[tile(D_MODEL), tile(D_SSM), tile(D_ATTN),
                  _resident((D_SSM, D_MODEL)), _resident((D_ATTN, D_MODEL)), _resident((1, D_MODEL)),
                  _resident((D_MODEL, D_FF)), _resident((D_MODEL, D_FF)), _resident((D_FF, D_MODEL)),
                  _resident((1, D_MODEL))],
        out_specs=tile(D_MODEL),
        out_shape=jax.ShapeDtypeStruct((n, D_MODEL), F32),
        compiler_params=pltpu.CompilerParams(
            dimension_semantics=("parallel",), vmem_limit_bytes=_vmem_limit(resident)),
        name="out_ffn2",
    )(x1, ys, ya, wos, woa, nw2, wg, wu, wd, nwf)


def _head_selection(offset):
    e = np.zeros((2 * LANES, D_SSM), np.float32)
    for h in range(SSM_HEADS):
        e[offset + h, h * SSM_HEAD_DIM:(h + 1) * SSM_HEAD_DIM] = 1.0
        e[LANES + offset + h, h * SSM_HEAD_DIM:(h + 1) * SSM_HEAD_DIM] = 1.0
    return jnp.asarray(e, BF16)


def _column_broadcast_selection():
    e = np.zeros((LANES, 2 * SSM_HEADS * CHUNK), np.float32)
    for t in range(3):
        for i in range(2 * SSM_HEADS):
            e[32 * t + i, i * CHUNK:(i + 1) * CHUNK] = 1.0
    return jnp.asarray(e, BF16)


def _prepare(norm_ffn1_w, ffn1_w_gate, ffn1_w_up, ffn1_w_down, norm_mix_w, w_in, conv_w, conv_b,
             dt_bias_fwd, dt_bias_bwd, a_log_fwd, a_log_bwd, d_skip, ssm_norm_w, attn_sink, w_out,
             norm_ffn2_w, ffn2_w_gate, ffn2_w_up, ffn2_w_down, norm_final_w):
    row = lambda v: v.reshape(1, -1).astype(F32)
    splits = np.cumsum((D_SSM, CONV_DIM, SSM_HEADS, SSM_HEADS, D_ATTN, D_KV))
    wz, wx, wdtf, wdtb, wq, wk, wv = jnp.split(w_in, splits, axis=1)
    wdt = jnp.concatenate([wdtf, wdtb, jnp.zeros((D_MODEL, DT_PAD - 2 * SSM_HEADS), F32)], axis=1)
    order = np.asarray(ATTN_HEAD_ORDER)
    cols = (order[:, None] * HEAD_DIM + np.arange(HEAD_DIM)[None, :]).reshape(-1)
    wq = wq[:, cols]
    pad_dt = jnp.zeros((DT_PAD - 2 * SSM_HEADS,), F32)
    slopes = jnp.exp2(-(8.0 / N_HEADS) * jnp.arange(1, N_HEADS + 1, dtype=F32))
    return dict(
        nw1=row(norm_ffn1_w), wg1=ffn1_w_gate.astype(BF16), wu1=ffn1_w_up.astype(BF16),
        wd1=ffn1_w_down.astype(BF16),
        nwm=row(norm_mix_w), wz=wz.astype(BF16), wx=wx.astype(BF16), wdt=wdt.astype(BF16),
        wq=wq.astype(BF16), wk=wk.astype(BF16), wv=wv.astype(BF16),
        cw=jnp.concatenate([conv_w, jnp.zeros((SUBLANES - CONV_K, CONV_DIM), F32)], axis=0),
        cb=row(conv_b),
        dt_bias=row(jnp.concatenate([dt_bias_fwd, dt_bias_bwd, pad_dt])),
        a_log=row(jnp.concatenate([a_log_fwd, a_log_bwd, pad_dt])),
        e2f=_head_selection(0), e2b=_head_selection(SSM_HEADS), e3=_column_broadcast_selection(),
        dskip=row(jnp.repeat(d_skip, SSM_HEAD_DIM)), nws=row(ssm_norm_w),
        slopes=slopes[order], sink=attn_sink.astype(F32)[order],
        wos=w_out[:D_SSM].astype(BF16), woa=w_out[D_SSM:][cols].astype(BF16),
        nw2=row(norm_ffn2_w), wg2=ffn2_w_gate.astype(BF16), wu2=ffn2_w_up.astype(BF16),
        wd2=ffn2_w_down.astype(BF16), nwf=row(norm_final_w),
    )


def _trunk(x, p):
    b, s, _ = x.shape
    assert s % SEQ_TILE == 0 and (b * s) % TOK_TILE == 0 and x.shape[2] == D_MODEL
    n = b * s
    x1 = _ffn1(x.reshape(n, D_MODEL), p["nw1"], p["wg1"], p["wu1"], p["wd1"])
    z, u, dt, q, k, v, prevb = _mix_in(
        x1.reshape(b, s, D_MODEL), p["nwm"], p["wz"], p["wx"], p["wdt"], p["wq"], p["wk"], p["wv"],
        p["cw"], p["cb"], p["dt_bias"], p["a_log"], p["e2b"])
    y_ssm = _ssd(u, dt, z, prevb, p["dt_bias"], p["a_log"], p["e2f"], p["e2b"], p["e3"],
                 p["dskip"], p["nws"])
    y_att = _attn(q, k, v, p["slopes"], p["sink"])
    out = _out_ffn2(x1, y_ssm.reshape(n, D_SSM), y_att.reshape(n, D_ATTN), p["wos"], p["woa"],
                    p["nw2"], p["wg2"], p["wu2"], p["wd2"], p["nwf"])
    return out.reshape(b, s, D_MODEL)


def kernel(x_prompt, x_sample, norm_ffn1_w, ffn1_w_gate, ffn1_w_up, ffn1_w_down, norm_mix_w, w_in, conv_w, conv_b, dt_bias_fwd, dt_bias_bwd, a_log_fwd, a_log_bwd, d_skip, ssm_norm_w, attn_sink, w_out, norm_ffn2_w, ffn2_w_gate, ffn2_w_up, ffn2_w_down, norm_final_w):
    p = _prepare(norm_ffn1_w[0], ffn1_w_gate[0], ffn1_w_up[0], ffn1_w_down[0], norm_mix_w[0], w_in[0],
                 conv_w[0], conv_b[0], dt_bias_fwd[0], dt_bias_bwd[0], a_log_fwd[0], a_log_bwd[0],
                 d_skip[0], ssm_norm_w[0], attn_sink[0], w_out[0], norm_ffn2_w[0], ffn2_w_gate[0],
                 ffn2_w_up[0], ffn2_w_down[0], norm_final_w)
    return (_trunk(x_prompt, p), _trunk(x_sample, p))
```

```python
import functools
import math

import jax
import jax.numpy as jnp
import numpy as np
from jax import lax
from jax.experimental import pallas as pl
from jax.experimental.pallas import tpu as pltpu

F32 = jnp.float32
BF16 = jnp.bfloat16

D_MODEL = 1024
D_FF = 2816
SSM_HEADS = 16
SSM_HEAD_DIM = 64
D_SSM = SSM_HEADS * SSM_HEAD_DIM
SSM_GROUPS = 2
D_STATE = 128
CONV_K = 5
CONV_DIM = D_SSM + 2 * SSM_GROUPS * D_STATE
N_HEADS = 16
KV_HEADS = 4
HEAD_DIM = 64
D_ATTN = N_HEADS * HEAD_DIM
D_KV = KV_HEADS * HEAD_DIM
WINDOW = 128
EPS = 1e-6
LOG2E = math.log2(math.e)
Q_SCALE = LOG2E / math.sqrt(HEAD_DIM)

LANES = 128
SUBLANES = 8
VMEM_BYTES_V7X = 64 * 1024 * 1024

CHUNK = LANES
TOK_TILE = 512
SEQ_TILE = 512
CHUNKS_PER_STEP = SEQ_TILE // CHUNK
MXU_TILE = 256
FF_SPLIT = (D_FF // MXU_TILE + 1) // 2 * MXU_TILE
FF_CHUNK = FF_SPLIT
HALO = SUBLANES
CONV_PIECE = MXU_TILE
DT_PAD = LANES
HEADS_PER_GROUP = SSM_HEADS // SSM_GROUPS
GROUP_W = HEADS_PER_GROUP * SSM_HEAD_DIM

ATTN_HEAD_ORDER = (0, 4, 1, 5, 2, 6, 3, 7, 8, 12, 9, 13, 10, 14, 11, 15)


def _vmem_limit(resident_bytes):
    return int(min(VMEM_BYTES_V7X - 8 * 1024 * 1024, 2 * resident_bytes + 8 * 1024 * 1024))


def _resident(shape):
    nd = len(shape)
    return pl.BlockSpec(shape, lambda *_: (0,) * nd, pipeline_mode=pl.Buffered(1))


def _dot(a, b):
    return jnp.dot(a, b, preferred_element_type=F32)


def _dot_nt(a, b):
    return lax.dot_general(a, b, (((1,), (1,)), ((), ())), preferred_element_type=F32)


def _dot_tn(a, b):
    return lax.dot_general(a, b, (((0,), (0,)), ((), ())), preferred_element_type=F32)


def _rmsnorm(x, w):
    return x * lax.rsqrt(jnp.mean(x * x, axis=-1, keepdims=True) + EPS) * w


def _silu(x):
    return x * jax.nn.sigmoid(x)


def _softplus(x):
    return jnp.maximum(x, 0.0) + jnp.log1p(jnp.exp(-jnp.abs(x)))


def _split2(x):
    hi = x.astype(BF16)
    lo = (x - hi.astype(F32)).astype(BF16)
    return hi, lo


def _split3(x):
    hi = x.astype(BF16)
    r = x - hi.astype(F32)
    mid = r.astype(BF16)
    lo = (r - mid.astype(F32)).astype(BF16)
    return hi, mid, lo


def _swiglu(xn, wg_ref, wu_ref, wd_ref):
    acc = None
    for cols in (slice(0, FF_SPLIT), slice(FF_SPLIT, D_FF)):
        g = _dot(xn, wg_ref[:, cols])
        u = _dot(xn, wu_ref[:, cols])
        part = _dot((_silu(g) * u).astype(BF16), wd_ref[cols, :])
        acc = part if acc is None else acc + part
    return acc


def _ffn1_kernel(x_ref, nw_ref, wg_ref, wu_ref, wd_ref, o_ref):
    x = x_ref[...]
    xn = _rmsnorm(x, nw_ref[...]).astype(BF16)
    o_ref[...] = x + 0.5 * _swiglu(xn, wg_ref, wu_ref, wd_ref)


def _ffn1(x2d, nw, wg, wu, wd):
    n = x2d.shape[0]
    tile = pl.BlockSpec((TOK_TILE, D_MODEL), lambda i: (i, 0))
    resident = 2 * 3 * D_MODEL * D_FF + 4 * 4 * TOK_TILE * D_MODEL + 3 * 4 * TOK_TILE * FF_CHUNK
    return pl.pallas_call(
        _ffn1_kernel,
        grid=(n // TOK_TILE,),
        in_specs=[tile, _resident((1, D_MODEL)), _resident((D_MODEL, D_FF)),
                  _resident((D_MODEL, D_FF)), _resident((D_FF, D_MODEL))],
        out_specs=tile,
        out_shape=jax.ShapeDtypeStruct((n, D_MODEL), F32),
        compiler_params=pltpu.CompilerParams(
            dimension_semantics=("parallel",), vmem_limit_bytes=_vmem_limit(resident)),
        name="ffn1",
    )(x2d, nw, wg, wu, wd)


def _tri(lower):
    r = lax.broadcasted_iota(jnp.int32, (CHUNK, CHUNK), 0)
    c = lax.broadcasted_iota(jnp.int32, (CHUNK, CHUNK), 1)
    return jnp.where((r >= c) if lower else (r <= c), 1.0, 0.0).astype(BF16)


def _tri_matmul(tri, x):
    hi, mid, lo = _split3(x)
    out = _dot(tri, jnp.concatenate([hi, mid, lo], axis=1))
    w = x.shape[1]
    return out[:, 0:w] + out[:, w:2 * w] + out[:, 2 * w:3 * w]


def _neg_exp_row(a_log_ref):
    lane = lax.broadcasted_iota(jnp.int32, (1, DT_PAD), 1)
    return jnp.where(lane < 2 * SSM_HEADS, -jnp.exp(a_log_ref[...]), 0.0)


def _expand_heads(x, e2_ref):
    hi, lo = _split2(x)
    return _dot(jnp.concatenate([hi, lo], axis=1), e2_ref[...])


def _mix_in_kernel(xm_ref, xp_ref, xn_ref, nw_ref, wz_ref, wx_ref, wdt_ref, wq_ref, wk_ref, wv_ref,
                   cw_ref, cb_ref, bias_ref, a_ref, e2b_ref,
                   z_ref, u_ref, dt_ref, q_ref, k_ref, v_ref, prev_ref, pad_ref, st_ref):
    j = pl.program_id(1)
    last = pl.num_programs(1) - 1

    @pl.when(j == 0)
    def _():
        st_ref[...] = jnp.zeros_like(st_ref)

    h = _rmsnorm(xm_ref[...], nw_ref[...]).astype(BF16)
    h_halo = _rmsnorm(jnp.concatenate([xp_ref[...], xn_ref[...]], axis=0), nw_ref[...]).astype(BF16)
    dt = _dot(h, wdt_ref[...])
    dt_ref[...] = dt

    def side_projection(piece):
        half = slice((piece % 2) * (D_SSM // 2), (piece % 2 + 1) * (D_SSM // 2))
        if piece < 2:
            z_ref[:, half] = _dot(h, wz_ref[:, half]).astype(BF16)
        elif piece < 4:
            q_ref[:, half] = (_dot(h, wq_ref[:, half]) * Q_SCALE).astype(BF16)
        elif piece == 4:
            k_ref[...] = _dot(h, wk_ref[...]).astype(BF16)
        else:
            v_ref[...] = _dot(h, wv_ref[...]).astype(BF16)

    first_tap = HALO - (CONV_K - 1) // 2
    for piece in range(CONV_DIM // CONV_PIECE):
        cols = slice(piece * CONV_PIECE, (piece + 1) * CONV_PIECE)
        xbc_halo = _dot(h_halo, wx_ref[:, cols])
        pad_ref[0:HALO, cols] = jnp.where(j < last, xbc_halo[0:HALO, :], 0.0)
        pad_ref[HALO:HALO + SEQ_TILE, cols] = _dot(h, wx_ref[:, cols])
        pad_ref[HALO + SEQ_TILE:2 * HALO + SEQ_TILE, cols] = jnp.where(j > 0, xbc_halo[HALO:2 * HALO, :], 0.0)
        for c in range(CHUNKS_PER_STEP):
            acc = cb_ref[:, cols]
            for k in range(CONV_K):
                start = c * CHUNK + first_tap + k
                acc = acc + pad_ref[start:start + CHUNK, cols] * cw_ref[k:k + 1, cols]
            u_ref[c * CHUNK:(c + 1) * CHUNK, cols] = _silu(acc).astype(BF16)
        side_projection(piece)

    tri_u = _tri(lower=False)
    a_row = _neg_exp_row(a_ref)
    chunk_rows = [slice(c * CHUNK, (c + 1) * CHUNK) for c in range(CHUNKS_PER_STEP)]
    dtvs = [_softplus(dt[r, :] + bias_ref[...]) for r in chunk_rows]
    rcss = [_tri_matmul(tri_u, dtv * a_row) for dtv in dtvs]
    w_exps = [_expand_heads(jnp.exp(rcs[0:1, :] - rcs) * dtv, e2b_ref) for rcs, dtv in zip(rcss, dtvs)]
    decs = [_expand_heads(jnp.broadcast_to(jnp.exp(rcs[0:1, :]), (SUBLANES, LANES)), e2b_ref)[0:1, :]
            for rcs in rcss]
    contribs = []
    for r, w_exp in zip(chunk_rows, w_exps):
        xsw = (u_ref[r, 0:D_SSM].astype(F32) * w_exp).astype(BF16)
        contribs.append([_dot_tn(u_ref[r, D_SSM + g * D_STATE:D_SSM + (g + 1) * D_STATE],
                                 xsw[:, g * GROUP_W:(g + 1) * GROUP_W]) for g in range(SSM_GROUPS)])
    for c in reversed(range(CHUNKS_PER_STEP)):
        prev_ref[c] = st_ref[...].astype(BF16)
        for g in range(SSM_GROUPS):
            cols = slice(g * GROUP_W, (g + 1) * GROUP_W)
            st_ref[:, cols] = st_ref[:, cols] * decs[c][:, cols] + contribs[c][g]


def _mix_in(x1, nw, wz, wx, wdt, wq, wk, wv, cw, cb, bias, a_log, e2b):
    b, s, _ = x1.shape
    steps = s // SEQ_TILE
    halo_per_tile = SEQ_TILE // HALO
    n_halo = s // HALO
    widths = (D_SSM, CONV_DIM, DT_PAD, D_ATTN, D_KV, D_KV)
    dtypes = (BF16, BF16, F32, BF16, BF16, BF16)
    rev = lambda bi, j: (bi, steps - 1 - j, 0)
    prv = lambda bi, j: (bi, jnp.maximum((steps - 1 - j) * halo_per_tile - 1, 0), 0)
    nxt = lambda bi, j: (bi, jnp.minimum((steps - j) * halo_per_tile, n_halo - 1), 0)
    resident = (2 * D_MODEL * sum(widths) + 2 * 2 * LANES * D_SSM
                + 2 * 4 * SEQ_TILE * (D_MODEL + sum(widths)) + 2 * 2 * SEQ_TILE * D_SSM
                + 4 * (SEQ_TILE + 2 * HALO) * CONV_DIM + 4 * D_STATE * D_SSM)
    return pl.pallas_call(
        _mix_in_kernel,
        grid=(b, steps),
        in_specs=[pl.BlockSpec((None, SEQ_TILE, D_MODEL), rev),
                  pl.BlockSpec((None, HALO, D_MODEL), prv),
                  pl.BlockSpec((None, HALO, D_MODEL), nxt),
                  _resident((1, D_MODEL))] + [_resident((D_MODEL, w)) for w in widths]
                 + [_resident((SUBLANES, CONV_DIM)), _resident((1, CONV_DIM)),
                    _resident((1, DT_PAD)), _resident((1, DT_PAD)), _resident((2 * LANES, D_SSM))],
        out_specs=[pl.BlockSpec((None, SEQ_TILE, w), rev) for w in widths]
                  + [pl.BlockSpec((None, CHUNKS_PER_STEP, D_STATE, D_SSM),
                                  lambda bi, j: (bi, steps - 1 - j, 0, 0))],
        out_shape=[jax.ShapeDtypeStruct((b, s, w), d) for w, d in zip(widths, dtypes)]
                  + [jax.ShapeDtypeStruct((b, s // CHUNK, D_STATE, D_SSM), BF16)],
        scratch_shapes=[pltpu.VMEM((SEQ_TILE + 2 * HALO, CONV_DIM), F32),
                        pltpu.VMEM((D_STATE, D_SSM), F32)],
        compiler_params=pltpu.CompilerParams(
            dimension_semantics=("arbitrary", "arbitrary"), vmem_limit_bytes=_vmem_limit(resident)),
        name="mix_in",
    )(x1, x1, x1, nw, wz, wx, wdt, wq, wk, wv, cw, cb, bias, a_log, e2b)


def _ssd_prologue(c, u_ref, dt_ref, bias_ref, a_row, e3_ref, tri_both, col):
    rows = slice(c * CHUNK, (c + 1) * CHUNK)
    dtv = _softplus(dt_ref[rows, :] + bias_ref[...])
    both = _tri_matmul(tri_both, dtv * a_row)
    cs = both[0:CHUNK, :]
    rcs = both[CHUNK:2 * CHUNK, :]
    comb = jnp.where(col < SSM_HEADS, cs, rcs)
    comb_t = comb.T
    dt_t = dtv.T

    hi, mid, lo = _split3(comb)
    packed = jnp.where(col < 32, hi.astype(F32),
                       jnp.where(col < 64, pltpu.roll(mid.astype(F32), 32, 1),
                                 jnp.where(col < 96, pltpu.roll(lo.astype(F32), 64, 1), 0.0)))
    colb = _dot(packed.astype(BF16), e3_ref[...])

    xs = u_ref[rows, 0:D_SSM].astype(F32)
    bmat = [u_ref[rows, D_SSM + g * D_STATE:D_SSM + (g + 1) * D_STATE] for g in range(SSM_GROUPS)]
    cmat = [u_ref[rows, D_SSM + (SSM_GROUPS + g) * D_STATE:D_SSM + (SSM_GROUPS + g + 1) * D_STATE]
            for g in range(SSM_GROUPS)]
    cb = [_dot_nt(cmat[g], bmat[g]) for g in range(SSM_GROUPS)]
    return dict(dtv=dtv, cs=cs, rcs=rcs, comb_t=comb_t, dt_t=dt_t, colb=colb, xs=xs,
                bmat=bmat, cmat=cmat, cb=cb)


def _ssd_head_pair(pro, j, row, col):
    half = SSM_HEAD_DIM
    colb, comb_t, dt_t = pro["colb"], pro["comb_t"], pro["dt_t"]
    g = (2 * j) // HEADS_PER_GROUP
    ms = []
    for h in (2 * j, 2 * j + 1):
        df = colb[:, h * CHUNK:(h + 1) * CHUNK] - comb_t[h:h + 1, :]
        db = colb[:, (SSM_HEADS + h) * CHUNK:(SSM_HEADS + h + 1) * CHUNK] \
            - comb_t[SSM_HEADS + h:SSM_HEADS + h + 1, :]
        dtf = dt_t[h:h + 1, :]
        dtb = dt_t[SSM_HEADS + h:SSM_HEADS + h + 1, :]
        dsel = jnp.where(row > col, dtf, jnp.where(row < col, dtb, dtf + dtb))
        decay = jnp.exp(jnp.where(row >= col, df, db))
        ms.append((pro["cb"][g] * decay * dsel).astype(BF16))
    xt = pro["xs"][:, j * LANES:(j + 1) * LANES]
    rhs = jnp.concatenate([jnp.where(col < half, xt, 0.0), jnp.where(col >= half, xt, 0.0)],
                          axis=0).astype(BF16)
    return _dot(jnp.concatenate(ms, axis=1), rhs)


def _ssd_epilogue(c, pro, ydiag, z_ref, prev_ref, e2f_ref, e2b_ref, dskip_ref, nw_ref, y_ref, st_ref):
    rows = slice(c * CHUNK, (c + 1) * CHUNK)
    cs, rcs, dtv, xs, bmat, cmat = (pro[n] for n in ("cs", "rcs", "dtv", "xs", "bmat", "cmat"))
    last = cs[CHUNK - 1:CHUNK, :]
    wf = _expand_heads(jnp.exp(last - cs) * dtv, e2f_ref)
    ef = _expand_heads(jnp.exp(cs), e2f_ref)
    eb = _expand_heads(jnp.exp(rcs), e2b_ref)
    decf = _expand_heads(jnp.broadcast_to(jnp.exp(last), (SUBLANES, LANES)), e2f_ref)[0:1, :]
    xsw = (xs * wf).astype(BF16)

    for g in range(SSM_GROUPS):
        cols = slice(g * GROUP_W, (g + 1) * GROUP_W)
        state = st_ref[:, cols]
        yoff_f = _dot(cmat[g], state.astype(BF16))
        yoff_b = _dot(cmat[g], prev_ref[c, :, cols])
        tiles = range(g * GROUP_W // LANES, (g + 1) * GROUP_W // LANES)
        y = jnp.concatenate([ydiag[t] for t in tiles], axis=1)
        y = y + ef[:, cols] * yoff_f + eb[:, cols] * yoff_b + xs[:, cols] * dskip_ref[:, cols]
        y = y * _silu(z_ref[rows, cols].astype(F32))
        y = y * lax.rsqrt(jnp.mean(y * y, axis=-1, keepdims=True) + EPS) * nw_ref[:, cols]
        y_ref[rows, cols] = y.astype(BF16)
        st_ref[:, cols] = state * decf[:, cols] + _dot_tn(bmat[g], xsw[:, cols])


ATTN_KEYS = 3 * CHUNK
Q_TILES_PER_KV_TILE = (D_ATTN // LANES) // (D_KV // LANES)


def _attn_prologue(c, k_refs, v_refs):
    keys = ATTN_KEYS
    half = HEAD_DIM
    rows = slice(c * CHUNK, (c + 1) * CHUNK)
    blk = pl.program_id(1) * CHUNKS_PER_STEP + c
    last_blk = pl.num_programs(1) * CHUNKS_PER_STEP - 1

    srow = lax.broadcasted_iota(jnp.int32, (keys, CHUNK), 0)
    tcol = lax.broadcasted_iota(jnp.int32, (keys, CHUNK), 1)
    idist = jnp.abs(tcol + CHUNK - srow)
    lo_key = jnp.where(blk == 0, CHUNK, 0)
    hi_key = jnp.where(blk == last_blk, 2 * CHUNK, keys)
    valid = (idist <= WINDOW) & (srow >= lo_key) & (srow < hi_key)
    neg_dist = jnp.where(valid, -idist.astype(F32), -jnp.inf)
    klane = lax.broadcasted_iota(jnp.int32, (keys, LANES), 1)
    vrow = lax.broadcasted_iota(jnp.int32, (LANES, keys), 0)
    srow16 = lax.broadcasted_iota(jnp.int32, (2 * SUBLANES, 2 * keys), 0)
    scol16 = lax.broadcasted_iota(jnp.int32, (2 * SUBLANES, 2 * keys), 1)
    sum_rows = jnp.where((srow16 < SUBLANES) == (scol16 < keys), 1.0, 0.0).astype(BF16)

    def window(refs, lanes):
        main_ref, prev_ref, next_ref = refs
        prev = prev_ref[:, lanes] if c == 0 else main_ref[(c - 1) * CHUNK:c * CHUNK, lanes]
        nxt = (next_ref[:, lanes] if c == CHUNKS_PER_STEP - 1
               else main_ref[(c + 1) * CHUNK:(c + 2) * CHUNK, lanes])
        return jnp.concatenate([prev, main_ref[rows, lanes], nxt], axis=0)

    kk, vv = [], []
    for m in range(D_KV // LANES):
        lanes = slice(m * LANES, (m + 1) * LANES)
        kf = window(k_refs, lanes).astype(F32)
        kk.append(jnp.concatenate([jnp.where(klane < half, kf, 0.0), jnp.where(klane >= half, kf, 0.0)],
                                  axis=0).astype(BF16))
        vt_t = window(v_refs, lanes).astype(F32).T
        vm = jnp.concatenate([jnp.where(vrow < half, vt_t, 0.0), jnp.where(vrow >= half, vt_t, 0.0)],
                             axis=1).astype(BF16)
        vv.append(jnp.concatenate([vm, sum_rows], axis=0))
    return dict(neg_dist=neg_dist, kk=kk, vv=vv)


def _attn_scores(c, apro, j, q_ref):
    rows = slice(c * CHUNK, (c + 1) * CHUNK)
    return _dot_nt(apro["kk"][j // Q_TILES_PER_KV_TILE], q_ref[rows, j * LANES:(j + 1) * LANES])


def _attn_finish(c, apro, j, st, slope_ref, sink_ref, o_ref):
    keys = ATTN_KEYS
    half = HEAD_DIM
    rows = slice(c * CHUNK, (c + 1) * CHUNK)
    orow = lax.broadcasted_iota(jnp.int32, (LANES, CHUNK), 0)
    ps, mxs = [], []
    for e in range(2):
        slot = 2 * j + e
        s = st[e * keys:(e + 1) * keys, :] + slope_ref[slot] * apro["neg_dist"]
        mx = jnp.maximum(jnp.max(s, axis=0, keepdims=True), sink_ref[slot] * LOG2E)
        ps.append(jnp.exp2(s - mx).astype(BF16))
        mxs.append(mx)
    ot = _dot(apro["vv"][j // Q_TILES_PER_KV_TILE], jnp.concatenate(ps, axis=0))
    inv = [1.0 / (ot[LANES + e * SUBLANES:LANES + e * SUBLANES + 1, :]
                  + jnp.exp2(sink_ref[2 * j + e] * LOG2E - mxs[e])) for e in range(2)]
    out = ot[0:LANES, :] * jnp.where(orow < half, inv[0], inv[1])
    o_ref[rows, j * LANES:(j + 1) * LANES] = out.T.astype(BF16)


def _mix_core_kernel(slope_ref, sink_ref, u_ref, dt_ref, z_ref, prev_ref, q_ref,
                     km_ref, kp_ref, kn_ref, vm_ref, vp_ref, vn_ref,
                     bias_ref, a_ref, e2f_ref, e2b_ref, e3_ref, dskip_ref, nw_ref,
                     ys_ref, ya_ref, st_ref):
    @pl.when(pl.program_id(1) == 0)
    def _():
        st_ref[...] = jnp.zeros_like(st_ref)

    tri_both = jnp.concatenate([_tri(lower=True), _tri(lower=False)], axis=0)
    row = lax.broadcasted_iota(jnp.int32, (CHUNK, CHUNK), 0)
    col = lax.broadcasted_iota(jnp.int32, (CHUNK, CHUNK), 1)
    a_row = _neg_exp_row(a_ref)
    n_pairs = SSM_HEADS // 2
    assert n_pairs == D_ATTN // LANES
    for c in range(CHUNKS_PER_STEP):
        pro = _ssd_prologue(c, u_ref, dt_ref, bias_ref, a_row, e3_ref, tri_both, col)
        apro = _attn_prologue(c, (km_ref, kp_ref, kn_ref), (vm_ref, vp_ref, vn_ref))
        ydiag = []
        st_next = _attn_scores(c, apro, 0, q_ref)
        for j in range(n_pairs):
            st = st_next
            if j + 1 < n_pairs:
                st_next = _attn_scores(c, apro, j + 1, q_ref)
            ydiag.append(_ssd_head_pair(pro, j, row, col))
            _attn_finish(c, apro, j, st, slope_ref, sink_ref, ya_ref)
        _ssd_epilogue(c, pro, ydiag, z_ref, prev_ref, e2f_ref, e2b_ref, dskip_ref, nw_ref, ys_ref, st_ref)


def _mix_core(u, dt, z, prevb, q, k, v, slopes2, sink, bias, a_log, e2f, e2b, e3, dskip, nw):
    b, s, _ = u.shape
    steps = s // SEQ_TILE
    nblk = s // CHUNK
    fwd = lambda bi, j: (bi, j, 0)
    prv = lambda bi, j: (bi, jnp.maximum(j * CHUNKS_PER_STEP - 1, 0), 0)
    nxt = lambda bi, j: (bi, jnp.minimum((j + 1) * CHUNKS_PER_STEP, nblk - 1), 0)
    smem = pl.BlockSpec(memory_space=pltpu.SMEM)
    kv_main = pl.BlockSpec((None, SEQ_TILE, D_KV), fwd)
    kv_prev = pl.BlockSpec((None, CHUNK, D_KV), prv)
    kv_next = pl.BlockSpec((None, CHUNK, D_KV), nxt)
    resident = (2 * 2 * SEQ_TILE * (CONV_DIM + 2 * D_SSM + D_ATTN + 2 * D_KV) + 2 * 4 * SEQ_TILE * DT_PAD
                + 2 * 2 * SEQ_TILE * (D_SSM + D_ATTN) + 4 * D_STATE * D_SSM
                + 2 * 2 * 2 * LANES * D_SSM + 2 * LANES * 2 * SSM_HEADS * CHUNK
                + 4 * CHUNK * 2 * SSM_HEADS * CHUNK + 16 * 4 * 3 * CHUNK * CHUNK)
    return pl.pallas_call(
        _mix_core_kernel,
        grid=(b, steps),
        in_specs=[smem, smem,
                  pl.BlockSpec((None, SEQ_TILE, CONV_DIM), fwd),
                  pl.BlockSpec((None, SEQ_TILE, DT_PAD), fwd),
                  pl.BlockSpec((None, SEQ_TILE, D_SSM), fwd),
                  pl.BlockSpec((None, CHUNKS_PER_STEP, D_STATE, D_SSM), lambda bi, j: (bi, j, 0, 0)),
                  pl.BlockSpec((None, SEQ_TILE, D_ATTN), fwd),
                  kv_main, kv_prev, kv_next, kv_main, kv_prev, kv_next,
                  _resident((1, DT_PAD)), _resident((1, DT_PAD)),
                  _resident((2 * LANES, D_SSM)), _resident((2 * LANES, D_SSM)),
                  _resident((LANES, 2 * SSM_HEADS * CHUNK)),
                  _resident((1, D_SSM)), _resident((1, D_SSM))],
        out_specs=[pl.BlockSpec((None, SEQ_TILE, D_SSM), fwd), pl.BlockSpec((None, SEQ_TILE, D_ATTN), fwd)],
        out_shape=[jax.ShapeDtypeStruct((b, s, D_SSM), BF16), jax.ShapeDtypeStruct((b, s, D_ATTN), BF16)],
        scratch_shapes=[pltpu.VMEM((D_STATE, D_SSM), F32)],
        compiler_params=pltpu.CompilerParams(
            dimension_semantics=("arbitrary", "arbitrary"), vmem_limit_bytes=_vmem_limit(resident)),
        name="mix_core",
    )(slopes2, sink, u, dt, z, prevb, q, k, k, k, v, v, v, bias, a_log, e2f, e2b, e3, dskip, nw)


def _out_ffn2_kernel(x_ref, ys_ref, ya_ref, wos_ref, woa_ref, nw2_ref, wg_ref, wu_ref, wd_ref,
                     nwf_ref, o_ref):
    x2 = x_ref[...] + _dot(ys_ref[...], wos_ref[...]) + _dot(ya_ref[...], woa_ref[...])
    xn = _rmsnorm(x2, nw2_ref[...]).astype(BF16)
    x3 = x2 + 0.5 * _swiglu(xn, wg_ref, wu_ref, wd_ref)
    o_ref[...] = _rmsnorm(x3, nwf_ref[...])


def _out_ffn2(x1, ys, ya, wos, woa, nw2, wg, wu, wd, nwf):
    n = x1.shape[0]

    def tile(w):
        return pl.BlockSpec((TOK_TILE, w), lambda i: (i, 0))

    resident = (2 * (3 * D_MODEL * D_FF + (D_SSM + D_ATTN) * D_MODEL)
                + 4 * 4 * TOK_TILE * D_MODEL + 2 * 2 * TOK_TILE * (D_SSM + D_ATTN)
                + 3 * 4 * TOK_TILE * FF_CHUNK)
    return pl.pallas_call(
        _out_ffn2_kernel,
        grid=(n // TOK_TILE,),
        in_specs=[tile(D_MODEL), tile(D_SSM), tile(D_ATTN),
                  _resident((D_SSM, D_MODEL)), _resident((D_ATTN, D_MODEL)), _resident((1, D_MODEL)),
                  _resident((D_MODEL, D_FF)), _resident((D_MODEL, D_FF)), _resident((D_FF, D_MODEL)),
                  _resident((1, D_MODEL))],
        out_specs=tile(D_MODEL),
        out_shape=jax.ShapeDtypeStruct((n, D_MODEL), F32),
        compiler_params=pltpu.CompilerParams(
            dimension_semantics=("parallel",), vmem_limit_bytes=_vmem_limit(resident)),
        name="out_ffn2",
    )(x1, ys, ya, wos, woa, nw2, wg, wu, wd, nwf)


def _head_selection(offset):
    e = np.zeros((2 * LANES, D_SSM), np.float32)
    for h in range(SSM_HEADS):
        e[offset + h, h * SSM_HEAD_DIM:(h + 1) * SSM_HEAD_DIM] = 1.0
        e[LANES + offset + h, h * SSM_HEAD_DIM:(h + 1) * SSM_HEAD_DIM] = 1.0
    return jnp.asarray(e, BF16)


def _column_broadcast_selection():
    e = np.zeros((LANES, 2 * SSM_HEADS * CHUNK), np.float32)
    for t in range(3):
        for i in range(2 * SSM_HEADS):
            e[32 * t + i, i * CHUNK:(i + 1) * CHUNK] = 1.0
    return jnp.asarray(e, BF16)


def _prepare(norm_ffn1_w, ffn1_w_gate, ffn1_w_up, ffn1_w_down, norm_mix_w, w_in, conv_w, conv_b,
             dt_bias_fwd, dt_bias_bwd, a_log_fwd, a_log_bwd, d_skip, ssm_norm_w, attn_sink, w_out,
             norm_ffn2_w, ffn2_w_gate, ffn2_w_up, ffn2_w_down, norm_final_w):
    row = lambda v: v.reshape(1, -1).astype(F32)
    splits = np.cumsum((D_SSM, CONV_DIM, SSM_HEADS, SSM_HEADS, D_ATTN, D_KV))
    wz, wx, wdtf, wdtb, wq, wk, wv = jnp.split(w_in, splits, axis=1)
    wdt = jnp.concatenate([wdtf, wdtb, jnp.zeros((D_MODEL, DT_PAD - 2 * SSM_HEADS), F32)], axis=1)
    order = np.asarray(ATTN_HEAD_ORDER)
    cols = (order[:, None] * HEAD_DIM + np.arange(HEAD_DIM)[None, :]).reshape(-1)
    wq = wq[:, cols]
    pad_dt = jnp.zeros((DT_PAD - 2 * SSM_HEADS,), F32)
    slopes = jnp.exp2(-(8.0 / N_HEADS) * jnp.arange(1, N_HEADS + 1, dtype=F32))
    return dict(
        nw1=row(norm_ffn1_w), wg1=ffn1_w_gate.astype(BF16), wu1=ffn1_w_up.astype(BF16),
        wd1=ffn1_w_down.astype(BF16),
        nwm=row(norm_mix_w), wz=wz.astype(BF16), wx=wx.astype(BF16), wdt=wdt.astype(BF16),
        wq=wq.astype(BF16), wk=wk.astype(BF16), wv=wv.astype(BF16),
        cw=jnp.concatenate([conv_w, jnp.zeros((SUBLANES - CONV_K, CONV_DIM), F32)], axis=0),
        cb=row(conv_b),
        dt_bias=row(jnp.concatenate([dt_bias_fwd, dt_bias_bwd, pad_dt])),
        a_log=row(jnp.concatenate([a_log_fwd, a_log_bwd, pad_dt])),
        e2f=_head_selection(0), e2b=_head_selection(SSM_HEADS), e3=_column_broadcast_selection(),
        dskip=row(jnp.repeat(d_skip, SSM_HEAD_DIM)), nws=row(ssm_norm_w),
        slopes2=(slopes * LOG2E)[order], sink=attn_sink.astype(F32)[order],
        wos=w_out[:D_SSM].astype(BF16), woa=w_out[D_SSM:][cols].astype(BF16),
        nw2=row(norm_ffn2_w), wg2=ffn2_w_gate.astype(BF16), wu2=ffn2_w_up.astype(BF16),
        wd2=ffn2_w_down.astype(BF16), nwf=row(norm_final_w),
    )


def _trunk(x, p):
    b, s, _ = x.shape
    assert s % SEQ_TILE == 0 and (b * s) % TOK_TILE == 0 and x.shape[2] == D_MODEL
    n = b * s
    x1 = _ffn1(x.reshape(n, D_MODEL), p["nw1"], p["wg1"], p["wu1"], p["wd1"])
    z, u, dt, q, k, v, prevb = _mix_in(
        x1.reshape(b, s, D_MODEL), p["nwm"], p["wz"], p["wx"], p["wdt"], p["wq"], p["wk"], p["wv"],
        p["cw"], p["cb"], p["dt_bias"], p["a_log"], p["e2b"])
    y_ssm, y_att = _mix_core(u, dt, z, prevb, q, k, v, p["slopes2"], p["sink"], p["dt_bias"], p["a_log"],
                             p["e2f"], p["e2b"], p["e3"], p["dskip"], p["nws"])
    out = _out_ffn2(x1, y_ssm.reshape(n, D_SSM), y_att.reshape(n, D_ATTN), p["wos"], p["woa"],
                    p["nw2"], p["wg2"], p["wu2"], p["wd2"], p["nwf"])
    return out.reshape(b, s, D_MODEL)


def kernel(x_prompt, x_sample, norm_ffn1_w, ffn1_w_gate, ffn1_w_up, ffn1_w_down, norm_mix_w, w_in, conv_w, conv_b, dt_bias_fwd, dt_bias_bwd, a_log_fwd, a_log_bwd, d_skip, ssm_norm_w, attn_sink, w_out, norm_ffn2_w, ffn2_w_gate, ffn2_w_up, ffn2_w_down, norm_final_w):
    p = _prepare(norm_ffn1_w[0], ffn1_w_gate[0], ffn1_w_up[0], ffn1_w_down[0], norm_mix_w[0], w_in[0],
                 conv_w[0], conv_b[0], dt_bias_fwd[0], dt_bias_bwd[0], a_log_fwd[0], a_log_bwd[0],
                 d_skip[0], ssm_norm_w[0], attn_sink[0], w_out[0], norm_ffn2_w[0], ffn2_w_gate[0],
                 ffn2_w_up[0], ffn2_w_down[0], norm_final_w)
    return (_trunk(x_prompt, p), _trunk(x_sample, p))
```

```python
import functools
import math

import jax
import jax.numpy as jnp
import numpy as np
from jax import lax
from jax.experimental import pallas as pl
from jax.experimental.pallas import tpu as pltpu

F32 = jnp.float32
BF16 = jnp.bfloat16

D_MODEL = 1024
D_FF = 2816
SSM_HEADS = 16
SSM_HEAD_DIM = 64
D_SSM = SSM_HEADS * SSM_HEAD_DIM
SSM_GROUPS = 2
D_STATE = 128
CONV_K = 5
CONV_DIM = D_SSM + 2 * SSM_GROUPS * D_STATE
N_HEADS = 16
KV_HEADS = 4
HEAD_DIM = 64
D_ATTN = N_HEADS * HEAD_DIM
D_KV = KV_HEADS * HEAD_DIM
WINDOW = 128
EPS = 1e-6
LOG2E = math.log2(math.e)
Q_SCALE = LOG2E / math.sqrt(HEAD_DIM)

LANES = 128
SUBLANES = 8
VMEM_BYTES_V7X = 64 * 1024 * 1024

CHUNK = LANES
TOK_TILE = 512
SEQ_TILE = 512
CHUNKS_PER_STEP = SEQ_TILE // CHUNK
MIX_TILE = 256
MIX_CHUNKS = MIX_TILE // CHUNK
MXU_TILE = 256
FF_SPLIT = (D_FF // MXU_TILE + 1) // 2 * MXU_TILE
FF_CHUNK = FF_SPLIT
HALO = SUBLANES
CONV_PIECE = MXU_TILE
DT_PAD = LANES
HEADS_PER_GROUP = SSM_HEADS // SSM_GROUPS
GROUP_W = HEADS_PER_GROUP * SSM_HEAD_DIM

ATTN_HEAD_ORDER = (0, 4, 1, 5, 2, 6, 3, 7, 8, 12, 9, 13, 10, 14, 11, 15)


def _vmem_limit(resident_bytes):
    return int(min(VMEM_BYTES_V7X - 8 * 1024 * 1024, 2 * resident_bytes + 8 * 1024 * 1024))


def _resident(shape):
    nd = len(shape)
    return pl.BlockSpec(shape, lambda *_: (0,) * nd, pipeline_mode=pl.Buffered(1))


def _dot(a, b):
    return jnp.dot(a, b, preferred_element_type=F32)


def _dot_nt(a, b):
    return lax.dot_general(a, b, (((1,), (1,)), ((), ())), preferred_element_type=F32)


def _dot_tn(a, b):
    return lax.dot_general(a, b, (((0,), (0,)), ((), ())), preferred_element_type=F32)


def _rmsnorm(x, w):
    return x * lax.rsqrt(jnp.mean(x * x, axis=-1, keepdims=True) + EPS) * w


def _silu(x):
    return x * jax.nn.sigmoid(x)


def _softplus(x):
    return jnp.maximum(x, 0.0) + jnp.log1p(jnp.exp(-jnp.abs(x)))


def _split2(x):
    hi = x.astype(BF16)
    lo = (x - hi.astype(F32)).astype(BF16)
    return hi, lo


def _split3(x):
    hi = x.astype(BF16)
    r = x - hi.astype(F32)
    mid = r.astype(BF16)
    lo = (r - mid.astype(F32)).astype(BF16)
    return hi, mid, lo


def _swiglu(xn, wg_ref, wu_ref, wd_ref):
    acc = None
    for cols in (slice(0, FF_SPLIT), slice(FF_SPLIT, D_FF)):
        g = _dot(xn, wg_ref[:, cols])
        u = _dot(xn, wu_ref[:, cols])
        part = _dot((_silu(g) * u).astype(BF16), wd_ref[cols, :])
        acc = part if acc is None else acc + part
    return acc


def _ffn1_kernel(x_ref, nw_ref, wg_ref, wu_ref, wd_ref, o_ref):
    x = x_ref[...]
    xn = _rmsnorm(x, nw_ref[...]).astype(BF16)
    o_ref[...] = x + 0.5 * _swiglu(xn, wg_ref, wu_ref, wd_ref)


def _ffn1(x2d, nw, wg, wu, wd):
    n = x2d.shape[0]
    tile = pl.BlockSpec((TOK_TILE, D_MODEL), lambda i: (i, 0))
    resident = 2 * 3 * D_MODEL * D_FF + 4 * 4 * TOK_TILE * D_MODEL + 3 * 4 * TOK_TILE * FF_CHUNK
    return pl.pallas_call(
        _ffn1_kernel,
        grid=(n // TOK_TILE,),
        in_specs=[tile, _resident((1, D_MODEL)), _resident((D_MODEL, D_FF)),
                  _resident((D_MODEL, D_FF)), _resident((D_FF, D_MODEL))],
        out_specs=tile,
        out_shape=jax.ShapeDtypeStruct((n, D_MODEL), F32),
        compiler_params=pltpu.CompilerParams(
            dimension_semantics=("parallel",), vmem_limit_bytes=_vmem_limit(resident)),
        name="ffn1",
    )(x2d, nw, wg, wu, wd)


def _tri(lower):
    r = lax.broadcasted_iota(jnp.int32, (CHUNK, CHUNK), 0)
    c = lax.broadcasted_iota(jnp.int32, (CHUNK, CHUNK), 1)
    return jnp.where((r >= c) if lower else (r <= c), 1.0, 0.0).astype(BF16)


def _tri_matmul(tri, x):
    hi, mid, lo = _split3(x)
    out = _dot(tri, jnp.concatenate([hi, mid, lo], axis=1))
    w = x.shape[1]
    return out[:, 0:w] + out[:, w:2 * w] + out[:, 2 * w:3 * w]


def _neg_exp_row(a_log_ref):
    lane = lax.broadcasted_iota(jnp.int32, (1, DT_PAD), 1)
    return jnp.where(lane < 2 * SSM_HEADS, -jnp.exp(a_log_ref[...]), 0.0)


def _expand_heads(x, e2_ref):
    hi, lo = _split2(x)
    return _dot(jnp.concatenate([hi, lo], axis=1), e2_ref[...])


def _mix_in_kernel(xm_ref, xp_ref, xn_ref, nw_ref, wz_ref, wx_ref, wdt_ref, wq_ref, wk_ref, wv_ref,
                   cw_ref, cb_ref, bias_ref, a_ref, e2b_ref,
                   z_ref, u_ref, dt_ref, q_ref, k_ref, v_ref, prev_ref, pad_ref, st_ref):
    j = pl.program_id(1)
    last = pl.num_programs(1) - 1

    @pl.when(j == 0)
    def _():
        st_ref[...] = jnp.zeros_like(st_ref)

    h = _rmsnorm(xm_ref[...], nw_ref[...]).astype(BF16)
    h_halo = _rmsnorm(jnp.concatenate([xp_ref[...], xn_ref[...]], axis=0), nw_ref[...]).astype(BF16)
    dt = _dot(h, wdt_ref[...])
    dt_ref[...] = dt

    def project(w_ref, o_ref, cols, scale=None):
        def run():
            y = _dot(h, w_ref[:, cols])
            o_ref[:, cols] = (y if scale is None else y * scale).astype(BF16)
        return run

    tiles = lambda width: [slice(t * MXU_TILE, (t + 1) * MXU_TILE) for t in range(width // MXU_TILE)]
    fillers = ([project(wz_ref, z_ref, cols) for cols in tiles(D_SSM)]
               + [project(wq_ref, q_ref, cols, Q_SCALE) for cols in tiles(D_ATTN)]
               + [project(wk_ref, k_ref, cols) for cols in tiles(D_KV)]
               + [project(wv_ref, v_ref, cols) for cols in tiles(D_KV)])

    def fill(count=1):
        for _ in range(count):
            if fillers:
                fillers.pop(0)()

    first_tap = HALO - (CONV_K - 1) // 2
    n_pieces = CONV_DIM // CONV_PIECE

    def project_conv_input(piece):
        cols = slice(piece * CONV_PIECE, (piece + 1) * CONV_PIECE)
        xbc_halo = _dot(h_halo, wx_ref[:, cols])
        pad_ref[0:HALO, cols] = jnp.where(j < last, xbc_halo[0:HALO, :], 0.0)
        pad_ref[HALO:HALO + SEQ_TILE, cols] = _dot(h, wx_ref[:, cols])
        pad_ref[HALO + SEQ_TILE:2 * HALO + SEQ_TILE, cols] = jnp.where(j > 0, xbc_halo[HALO:2 * HALO, :], 0.0)

    def conv(piece):
        cols = slice(piece * CONV_PIECE, (piece + 1) * CONV_PIECE)
        for c in range(CHUNKS_PER_STEP):
            acc = cb_ref[:, cols]
            for k in range(CONV_K):
                start = c * CHUNK + first_tap + k
                acc = acc + pad_ref[start:start + CHUNK, cols] * cw_ref[k:k + 1, cols]
            u_ref[c * CHUNK:(c + 1) * CHUNK, cols] = _silu(acc).astype(BF16)

    project_conv_input(0)
    for piece in range(n_pieces):
        if piece + 1 < n_pieces:
            project_conv_input(piece + 1)
        conv(piece)
        fill()

    tri_u = _tri(lower=False)
    a_row = _neg_exp_row(a_ref)
    chunk_rows = [slice(c * CHUNK, (c + 1) * CHUNK) for c in range(CHUNKS_PER_STEP)]
    dtvs = [_softplus(dt[r, :] + bias_ref[...]) for r in chunk_rows]
    rcss = [_tri_matmul(tri_u, dtv * a_row) for dtv in dtvs]
    fill()
    w_exps = [_expand_heads(jnp.exp(rcs[0:1, :] - rcs) * dtv, e2b_ref) for rcs, dtv in zip(rcss, dtvs)]
    decs = [_expand_heads(jnp.broadcast_to(jnp.exp(rcs[0:1, :]), (SUBLANES, LANES)), e2b_ref)[0:1, :]
            for rcs in rcss]
    fill()
    contribs = []
    for r, w_exp in zip(chunk_rows, w_exps):
        xsw = (u_ref[r, 0:D_SSM].astype(F32) * w_exp).astype(BF16)
        contribs.append([_dot_tn(u_ref[r, D_SSM + g * D_STATE:D_SSM + (g + 1) * D_STATE],
                                 xsw[:, g * GROUP_W:(g + 1) * GROUP_W]) for g in range(SSM_GROUPS)])
        fill()
    fill(len(fillers))
    for c in reversed(range(CHUNKS_PER_STEP)):
        prev_ref[c] = st_ref[...].astype(BF16)
        for g in range(SSM_GROUPS):
            cols = slice(g * GROUP_W, (g + 1) * GROUP_W)
            st_ref[:, cols] = st_ref[:, cols] * decs[c][:, cols] + contribs[c][g]


def _mix_in(x1, nw, wz, wx, wdt, wq, wk, wv, cw, cb, bias, a_log, e2b):
    b, s, _ = x1.shape
    steps = s // SEQ_TILE
    halo_per_tile = SEQ_TILE // HALO
    n_halo = s // HALO
    widths = (D_SSM, CONV_DIM, DT_PAD, D_ATTN, D_KV, D_KV)
    dtypes = (BF16, BF16, F32, BF16, BF16, BF16)
    rev = lambda bi, j: (bi, steps - 1 - j, 0)
    prv = lambda bi, j: (bi, jnp.maximum((steps - 1 - j) * halo_per_tile - 1, 0), 0)
    nxt = lambda bi, j: (bi, jnp.minimum((steps - j) * halo_per_tile, n_halo - 1), 0)
    resident = (2 * D_MODEL * sum(widths) + 2 * 2 * LANES * D_SSM
                + 2 * 4 * SEQ_TILE * (D_MODEL + sum(widths)) + 2 * 2 * SEQ_TILE * D_SSM
                + 4 * (SEQ_TILE + 2 * HALO) * CONV_DIM + 4 * D_STATE * D_SSM)
    return pl.pallas_call(
        _mix_in_kernel,
        grid=(b, steps),
        in_specs=[pl.BlockSpec((None, SEQ_TILE, D_MODEL), rev),
                  pl.BlockSpec((None, HALO, D_MODEL), prv),
                  pl.BlockSpec((None, HALO, D_MODEL), nxt),
                  _resident((1, D_MODEL))] + [_resident((D_MODEL, w)) for w in widths]
                 + [_resident((SUBLANES, CONV_DIM)), _resident((1, CONV_DIM)),
                    _resident((1, DT_PAD)), _resident((1, DT_PAD)), _resident((2 * LANES, D_SSM))],
        out_specs=[pl.BlockSpec((None, SEQ_TILE, w), rev) for w in widths]
                  + [pl.BlockSpec((None, CHUNKS_PER_STEP, D_STATE, D_SSM),
                                  lambda bi, j: (bi, steps - 1 - j, 0, 0))],
        out_shape=[jax.ShapeDtypeStruct((b, s, w), d) for w, d in zip(widths, dtypes)]
                  + [jax.ShapeDtypeStruct((b, s // CHUNK, D_STATE, D_SSM), BF16)],
        scratch_shapes=[pltpu.VMEM((SEQ_TILE + 2 * HALO, CONV_DIM), F32),
                        pltpu.VMEM((D_STATE, D_SSM), F32)],
        compiler_params=pltpu.CompilerParams(
            dimension_semantics=("arbitrary", "arbitrary"), vmem_limit_bytes=_vmem_limit(resident)),
        name="mix_in",
    )(x1, x1, x1, nw, wz, wx, wdt, wq, wk, wv, cw, cb, bias, a_log, e2b)


def _ssd_prologue(c, u_ref, dt_ref, bias_ref, a_row, tri_both, col):
    rows = slice(c * CHUNK, (c + 1) * CHUNK)
    dtv = _softplus(dt_ref[rows, :] + bias_ref[...])
    both = _tri_matmul(tri_both, dtv * a_row)
    cs = both[0:CHUNK, :]
    rcs = both[CHUNK:2 * CHUNK, :]
    comb = jnp.where(col < SSM_HEADS, cs, rcs)
    comb_t = comb.T
    dt_t = dtv.T

    xs = u_ref[rows, 0:D_SSM].astype(F32)
    bmat = [u_ref[rows, D_SSM + g * D_STATE:D_SSM + (g + 1) * D_STATE] for g in range(SSM_GROUPS)]
    cmat = [u_ref[rows, D_SSM + (SSM_GROUPS + g) * D_STATE:D_SSM + (SSM_GROUPS + g + 1) * D_STATE]
            for g in range(SSM_GROUPS)]
    cb = [_dot_nt(cmat[g], bmat[g]) for g in range(SSM_GROUPS)]
    return dict(dtv=dtv, comb=comb, comb_t=comb_t, dt_t=dt_t, xs=xs, bmat=bmat, cmat=cmat, cb=cb)


def _ssd_head_pair(pro, j, row, col):
    half = SSM_HEAD_DIM
    comb, comb_t, dt_t, dtv = pro["comb"], pro["comb_t"], pro["dt_t"], pro["dtv"]
    g = (2 * j) // HEADS_PER_GROUP
    lane_bcast = lambda a, i: jnp.broadcast_to(a[:, i:i + 1], (CHUNK, CHUNK))
    ms, wf, ef, eb, decf = [], [], [], [], []
    for h in (2 * j, 2 * j + 1):
        csf = lane_bcast(comb, h)
        rcsb = lane_bcast(comb, SSM_HEADS + h)
        df = csf - comb_t[h:h + 1, :]
        db = rcsb - comb_t[SSM_HEADS + h:SSM_HEADS + h + 1, :]
        dtf = dt_t[h:h + 1, :]
        dtb = dt_t[SSM_HEADS + h:SSM_HEADS + h + 1, :]
        dsel = jnp.where(row > col, dtf, jnp.where(row < col, dtb, dtf + dtb))
        decay = jnp.exp(jnp.where(row >= col, df, db))
        ms.append((pro["cb"][g] * decay * dsel).astype(BF16))
        last = csf[CHUNK - 1:CHUNK, :]
        wf.append(jnp.exp(last - csf) * lane_bcast(dtv, h))
        ef.append(jnp.exp(csf))
        eb.append(jnp.exp(rcsb))
        decf.append(jnp.exp(last))
    pair = lambda ab: jnp.where(col[0:ab[0].shape[0], :] < half, ab[0], ab[1])
    xt = pro["xs"][:, j * LANES:(j + 1) * LANES]
    rhs = jnp.concatenate([jnp.where(col < half, xt, 0.0), jnp.where(col >= half, xt, 0.0)],
                          axis=0).astype(BF16)
    ydiag = _dot(jnp.concatenate(ms, axis=1), rhs)
    return dict(ydiag=ydiag, wf=pair(wf), ef=pair(ef), eb=pair(eb), decf=pair(decf))


def _ssd_epilogue(c, pro, pairs, z_ref, prev_ref, dskip_ref, nw_ref, y_ref, st_ref):
    rows = slice(c * CHUNK, (c + 1) * CHUNK)
    xs, bmat, cmat = (pro[n] for n in ("xs", "bmat", "cmat"))
    ydiag, wf, ef, eb, decf = (jnp.concatenate([p[n] for p in pairs], axis=1)
                               for n in ("ydiag", "wf", "ef", "eb", "decf"))
    xsw = (xs * wf).astype(BF16)

    for g in range(SSM_GROUPS):
        cols = slice(g * GROUP_W, (g + 1) * GROUP_W)
        state = st_ref[:, cols]
        yoff_f = _dot(cmat[g], state.astype(BF16))
        yoff_b = _dot(cmat[g], prev_ref[c, :, cols])
        y = (ydiag[:, cols] + ef[:, cols] * yoff_f + eb[:, cols] * yoff_b
             + xs[:, cols] * dskip_ref[:, cols])
        y = y * _silu(z_ref[rows, cols].astype(F32))
        y = y * lax.rsqrt(jnp.mean(y * y, axis=-1, keepdims=True) + EPS) * nw_ref[:, cols]
        y_ref[rows, cols] = y.astype(BF16)
        st_ref[:, cols] = state * decf[:, cols] + _dot_tn(bmat[g], xsw[:, cols])


ATTN_KEYS = 3 * CHUNK
Q_TILES_PER_KV_TILE = (D_ATTN // LANES) // (D_KV // LANES)


def _attn_prologue(c, n_chunks, blk, last_blk, k_refs, v_refs):
    keys = ATTN_KEYS
    half = HEAD_DIM
    rows = slice(c * CHUNK, (c + 1) * CHUNK)

    srow = lax.broadcasted_iota(jnp.int32, (keys, CHUNK), 0)
    tcol = lax.broadcasted_iota(jnp.int32, (keys, CHUNK), 1)
    idist = jnp.abs(tcol + CHUNK - srow)
    lo_key = jnp.where(blk == 0, CHUNK, 0)
    hi_key = jnp.where(blk == last_blk, 2 * CHUNK, keys)
    valid = (idist <= WINDOW) & (srow >= lo_key) & (srow < hi_key)
    neg_dist = jnp.where(valid, -idist.astype(F32), -jnp.inf)
    klane = lax.broadcasted_iota(jnp.int32, (keys, LANES), 1)
    vrow = lax.broadcasted_iota(jnp.int32, (LANES, keys), 0)
    srow16 = lax.broadcasted_iota(jnp.int32, (2 * SUBLANES, 2 * keys), 0)
    scol16 = lax.broadcasted_iota(jnp.int32, (2 * SUBLANES, 2 * keys), 1)
    sum_rows = jnp.where((srow16 < SUBLANES) == (scol16 < keys), 1.0, 0.0).astype(BF16)

    def window(refs, lanes):
        main_ref, prev_ref, next_ref = refs
        prev = prev_ref[:, lanes] if c == 0 else main_ref[(c - 1) * CHUNK:c * CHUNK, lanes]
        nxt = (next_ref[:, lanes] if c == n_chunks - 1
               else main_ref[(c + 1) * CHUNK:(c + 2) * CHUNK, lanes])
        return jnp.concatenate([prev, main_ref[rows, lanes], nxt], axis=0)

    kk, vv = [], []
    for m in range(D_KV // LANES):
        lanes = slice(m * LANES, (m + 1) * LANES)
        kf = window(k_refs, lanes).astype(F32)
        kk.append(jnp.concatenate([jnp.where(klane < half, kf, 0.0), jnp.where(klane >= half, kf, 0.0)],
                                  axis=0).astype(BF16))
        vt_t = window(v_refs, lanes).astype(F32).T
        vm = jnp.concatenate([jnp.where(vrow < half, vt_t, 0.0), jnp.where(vrow >= half, vt_t, 0.0)],
                             axis=1).astype(BF16)
        vv.append(jnp.concatenate([vm, sum_rows], axis=0))
    return dict(neg_dist=neg_dist, kk=kk, vv=vv)


Q_TILES_PER_CALL = MXU_TILE // LANES
assert Q_TILES_PER_KV_TILE % Q_TILES_PER_CALL == 0


def _attn_scores(c, apro, i, q_ref):
    rows = slice(c * CHUNK, (c + 1) * CHUNK)
    kv_tile = i * Q_TILES_PER_CALL // Q_TILES_PER_KV_TILE
    q_stack = jnp.concatenate([q_ref[rows, j * LANES:(j + 1) * LANES]
                               for j in range(Q_TILES_PER_CALL * i, Q_TILES_PER_CALL * (i + 1))], axis=0)
    return _dot_nt(apro["kk"][kv_tile], q_stack)


def _attn_finish(c, apro, i, st, slope_ref, sink_ref, o_ref):
    keys = ATTN_KEYS
    half = HEAD_DIM
    rows = slice(c * CHUNK, (c + 1) * CHUNK)
    kv_tile = i * Q_TILES_PER_CALL // Q_TILES_PER_KV_TILE
    orow = lax.broadcasted_iota(jnp.int32, (LANES, CHUNK), 0)
    ps, mxs = [], []
    for t in range(Q_TILES_PER_CALL):
        p_tile = []
        for e in range(2):
            slot = 2 * (Q_TILES_PER_CALL * i + t) + e
            s = st[e * keys:(e + 1) * keys, t * CHUNK:(t + 1) * CHUNK] + slope_ref[slot] * apro["neg_dist"]
            mx = jnp.maximum(jnp.max(s, axis=0, keepdims=True), sink_ref[slot] * LOG2E)
            p_tile.append(jnp.exp2(s - mx).astype(BF16))
            mxs.append(mx)
        ps.append(jnp.concatenate(p_tile, axis=0))
    ot = _dot(apro["vv"][kv_tile], jnp.concatenate(ps, axis=1))
    for t in range(Q_TILES_PER_CALL):
        j = Q_TILES_PER_CALL * i + t
        ot_t = ot[:, t * CHUNK:(t + 1) * CHUNK]
        inv = [1.0 / (ot_t[LANES + e * SUBLANES:LANES + e * SUBLANES + 1, :]
                      + jnp.exp2(sink_ref[2 * j + e] * LOG2E - mxs[2 * t + e])) for e in range(2)]
        out = ot_t[0:LANES, :] * jnp.where(orow < half, inv[0], inv[1])
        o_ref[rows, j * LANES:(j + 1) * LANES] = out.T.astype(BF16)


def _out_ffn2_slices(ys, ya, x_ref, wos_ref, woa_ref, nw2_ref, wg_ref, wu_ref, wd_ref, nwf_ref, o_ref):
    v = {}
    tile = lambda t: slice(t * MXU_TILE, (t + 1) * MXU_TILE)

    def out_proj(y, w_ref, t, first):
        def run():
            base = x_ref[...] if first else v["x2"]
            v["x2"] = base + _dot(y[:, tile(t)], w_ref[tile(t), :])
        return run

    def gate_up(t):
        def run():
            if "xn" not in v:
                v["xn"] = _rmsnorm(v["x2"], nw2_ref[...]).astype(BF16)
            v["g", t] = _dot(v["xn"], wg_ref[:, tile(t)])
            v["u", t] = _dot(v["xn"], wu_ref[:, tile(t)])
        return run

    def down(t):
        def run():
            hid = (_silu(v.pop(("g", t))) * v.pop(("u", t))).astype(BF16)
            part = _dot(hid, wd_ref[tile(t), :])
            v["acc"] = part if "acc" not in v else v["acc"] + part
        return run

    def finish():
        o_ref[...] = _rmsnorm(v["x2"] + 0.5 * v["acc"], nwf_ref[...])

    n_y = D_SSM // MXU_TILE
    n_ff = D_FF // MXU_TILE
    items = ([out_proj(ys, wos_ref, t, t == 0) for t in range(n_y)]
             + [out_proj(ya, woa_ref, t, False) for t in range(D_ATTN // MXU_TILE)])
    for t in range(n_ff):
        items.append(gate_up(t))
        if t > 0:
            items.append(down(t - 1))
    items += [down(n_ff - 1), finish]
    return items


def _mix_out_kernel(slope_ref, sink_ref, u_ref, dt_ref, z_ref, prev_ref, q_ref,
                    km_ref, kp_ref, kn_ref, vm_ref, vp_ref, vn_ref, x_ref,
                    bias_ref, a_ref, dskip_ref, nw_ref,
                    wos_ref, woa_ref, nw2_ref, wg_ref, wu_ref, wd_ref, nwf_ref,
                    o_ref, ys_ref, ya_ref, st_ref):
    j = pl.program_id(1)
    n_tiles = pl.num_programs(1) - 1

    @pl.when(j == 0)
    def _():
        st_ref[...] = jnp.zeros_like(st_ref)
        ys_ref[...] = jnp.zeros_like(ys_ref)
        ya_ref[...] = jnp.zeros_like(ya_ref)

    fillers = _out_ffn2_slices(ys_ref[...], ya_ref[...], x_ref, wos_ref, woa_ref, nw2_ref,
                               wg_ref, wu_ref, wd_ref, nwf_ref, o_ref)

    def fill(count=1):
        for _ in range(count):
            if fillers:
                fillers.pop(0)()

    tile_idx = jnp.minimum(j, n_tiles - 1)
    tri_both = jnp.concatenate([_tri(lower=True), _tri(lower=False)], axis=0)
    row = lax.broadcasted_iota(jnp.int32, (CHUNK, CHUNK), 0)
    col = lax.broadcasted_iota(jnp.int32, (CHUNK, CHUNK), 1)
    a_row = _neg_exp_row(a_ref)
    n_pairs = SSM_HEADS // 2
    assert n_pairs == D_ATTN // LANES
    for c in range(MIX_CHUNKS):
        pro = _ssd_prologue(c, u_ref, dt_ref, bias_ref, a_row, tri_both, col)
        fill()
        apro = _attn_prologue(c, MIX_CHUNKS, tile_idx * MIX_CHUNKS + c, n_tiles * MIX_CHUNKS - 1,
                              (km_ref, kp_ref, kn_ref), (vm_ref, vp_ref, vn_ref))
        fill()
        ydiag = []
        n_calls = n_pairs // Q_TILES_PER_CALL
        st_next = _attn_scores(c, apro, 0, q_ref)
        for i in range(n_calls):
            st = st_next
            if i + 1 < n_calls:
                st_next = _attn_scores(c, apro, i + 1, q_ref)
            fill()
            for t in range(Q_TILES_PER_CALL):
                ydiag.append(_ssd_head_pair(pro, Q_TILES_PER_CALL * i + t, row, col))
                fill()
            _attn_finish(c, apro, i, st, slope_ref, sink_ref, ya_ref)
        _ssd_epilogue(c, pro, ydiag, z_ref, prev_ref, dskip_ref, nw_ref, ys_ref, st_ref)
        fill()
    fill(len(fillers))


def _mix_out(u, dt, z, prevb, q, k, v, x1, slopes2, sink, bias, a_log, dskip, nw,
             wos, woa, nw2, wg, wu, wd, nwf):
    b, s, _ = u.shape
    n_tiles = s // MIX_TILE
    nblk = s // CHUNK
    cur = lambda j: jnp.minimum(j, n_tiles - 1)
    mix = lambda bi, j: (bi, cur(j), 0)
    lag = lambda bi, j: (bi, jnp.maximum(j - 1, 0), 0)
    prv = lambda bi, j: (bi, jnp.maximum(cur(j) * MIX_CHUNKS - 1, 0), 0)
    nxt = lambda bi, j: (bi, jnp.minimum((cur(j) + 1) * MIX_CHUNKS, nblk - 1), 0)
    smem = pl.BlockSpec(memory_space=pltpu.SMEM)
    kv_main = pl.BlockSpec((None, MIX_TILE, D_KV), mix)
    kv_prev = pl.BlockSpec((None, CHUNK, D_KV), prv)
    kv_next = pl.BlockSpec((None, CHUNK, D_KV), nxt)
    resident = (2 * (3 * D_MODEL * D_FF + (D_SSM + D_ATTN) * D_MODEL)
                + 2 * 2 * MIX_TILE * (CONV_DIM + 2 * D_SSM + D_ATTN + 2 * D_KV) + 2 * 4 * MIX_TILE * DT_PAD
                + 2 * 2 * 4 * MIX_TILE * D_MODEL + 2 * MIX_TILE * (D_SSM + D_ATTN) + 4 * D_STATE * D_SSM)
    return pl.pallas_call(
        _mix_out_kernel,
        grid=(b, n_tiles + 1),
        in_specs=[smem, smem,
                  pl.BlockSpec((None, MIX_TILE, CONV_DIM), mix),
                  pl.BlockSpec((None, MIX_TILE, DT_PAD), mix),
                  pl.BlockSpec((None, MIX_TILE, D_SSM), mix),
                  pl.BlockSpec((None, MIX_CHUNKS, D_STATE, D_SSM), lambda bi, j: (bi, cur(j), 0, 0)),
                  pl.BlockSpec((None, MIX_TILE, D_ATTN), mix),
                  kv_main, kv_prev, kv_next, kv_main, kv_prev, kv_next,
                  pl.BlockSpec((None, MIX_TILE, D_MODEL), lag),
                  _resident((1, DT_PAD)), _resident((1, DT_PAD)),
                  _resident((1, D_SSM)), _resident((1, D_SSM)),
                  _resident((D_SSM, D_MODEL)), _resident((D_ATTN, D_MODEL)), _resident((1, D_MODEL)),
                  _resident((D_MODEL, D_FF)), _resident((D_MODEL, D_FF)), _resident((D_FF, D_MODEL)),
                  _resident((1, D_MODEL))],
        out_specs=pl.BlockSpec((None, MIX_TILE, D_MODEL), lag),
        out_shape=jax.ShapeDtypeStruct((b, s, D_MODEL), F32),
        scratch_shapes=[pltpu.VMEM((MIX_TILE, D_SSM), BF16), pltpu.VMEM((MIX_TILE, D_ATTN), BF16),
                        pltpu.VMEM((D_STATE, D_SSM), F32)],
        compiler_params=pltpu.CompilerParams(
            dimension_semantics=("arbitrary", "arbitrary"), vmem_limit_bytes=_vmem_limit(resident)),
        name="mix_out",
    )(slopes2, sink, u, dt, z, prevb, q, k, k, k, v, v, v, x1, bias, a_log, dskip, nw,
      wos, woa, nw2, wg, wu, wd, nwf)


def _head_selection(offset):
    e = np.zeros((2 * LANES, D_SSM), np.float32)
    for h in range(SSM_HEADS):
        e[offset + h, h * SSM_HEAD_DIM:(h + 1) * SSM_HEAD_DIM] = 1.0
        e[LANES + offset + h, h * SSM_HEAD_DIM:(h + 1) * SSM_HEAD_DIM] = 1.0
    return jnp.asarray(e, BF16)


def _prepare(norm_ffn1_w, ffn1_w_gate, ffn1_w_up, ffn1_w_down, norm_mix_w, w_in, conv_w, conv_b,
             dt_bias_fwd, dt_bias_bwd, a_log_fwd, a_log_bwd, d_skip, ssm_norm_w, attn_sink, w_out,
             norm_ffn2_w, ffn2_w_gate, ffn2_w_up, ffn2_w_down, norm_final_w):
    row = lambda v: v.reshape(1, -1).astype(F32)
    splits = np.cumsum((D_SSM, CONV_DIM, SSM_HEADS, SSM_HEADS, D_ATTN, D_KV))
    wz, wx, wdtf, wdtb, wq, wk, wv = jnp.split(w_in, splits, axis=1)
    wdt = jnp.concatenate([wdtf, wdtb, jnp.zeros((D_MODEL, DT_PAD - 2 * SSM_HEADS), F32)], axis=1)
    order = np.asarray(ATTN_HEAD_ORDER)
    cols = (order[:, None] * HEAD_DIM + np.arange(HEAD_DIM)[None, :]).reshape(-1)
    wq = wq[:, cols]
    pad_dt = jnp.zeros((DT_PAD - 2 * SSM_HEADS,), F32)
    slopes = jnp.exp2(-(8.0 / N_HEADS) * jnp.arange(1, N_HEADS + 1, dtype=F32))
    return dict(
        nw1=row(norm_ffn1_w), wg1=ffn1_w_gate.astype(BF16), wu1=ffn1_w_up.astype(BF16),
        wd1=ffn1_w_down.astype(BF16),
        nwm=row(norm_mix_w), wz=wz.astype(BF16), wx=wx.astype(BF16), wdt=wdt.astype(BF16),
        wq=wq.astype(BF16), wk=wk.astype(BF16), wv=wv.astype(BF16),
        cw=jnp.concatenate([conv_w, jnp.zeros((SUBLANES - CONV_K, CONV_DIM), F32)], axis=0),
        cb=row(conv_b),
        dt_bias=row(jnp.concatenate([dt_bias_fwd, dt_bias_bwd, pad_dt])),
        a_log=row(jnp.concatenate([a_log_fwd, a_log_bwd, pad_dt])),
        e2b=_head_selection(SSM_HEADS),
        dskip=row(jnp.repeat(d_skip, SSM_HEAD_DIM)), nws=row(ssm_norm_w),
        slopes2=(slopes * LOG2E)[order], sink=attn_sink.astype(F32)[order],
        wos=w_out[:D_SSM].astype(BF16), woa=w_out[D_SSM:][cols].astype(BF16),
        nw2=row(norm_ffn2_w), wg2=ffn2_w_gate.astype(BF16), wu2=ffn2_w_up.astype(BF16),
        wd2=ffn2_w_down.astype(BF16), nwf=row(norm_final_w),
    )


def _trunk(x, p):
    b, s, _ = x.shape
    assert s % SEQ_TILE == 0 and (b * s) % TOK_TILE == 0 and x.shape[2] == D_MODEL
    n = b * s
    x1 = _ffn1(x.reshape(n, D_MODEL), p["nw1"], p["wg1"], p["wu1"], p["wd1"])
    z, u, dt, q, k, v, prevb = _mix_in(
        x1.reshape(b, s, D_MODEL), p["nwm"], p["wz"], p["wx"], p["wdt"], p["wq"], p["wk"], p["wv"],
        p["cw"], p["cb"], p["dt_bias"], p["a_log"], p["e2b"])
    return _mix_out(u, dt, z, prevb, q, k, v, x1.reshape(b, s, D_MODEL), p["slopes2"], p["sink"],
                    p["dt_bias"], p["a_log"], p["dskip"], p["nws"],
                    p["wos"], p["woa"], p["nw2"], p["wg2"], p["wu2"], p["wd2"], p["nwf"])


def kernel(x_prompt, x_sample, norm_ffn1_w, ffn1_w_gate, ffn1_w_up, ffn1_w_down, norm_mix_w, w_in, conv_w, conv_b, dt_bias_fwd, dt_bias_bwd, a_log_fwd, a_log_bwd, d_skip, ssm_norm_w, attn_sink, w_out, norm_ffn2_w, ffn2_w_gate, ffn2_w_up, ffn2_w_down, norm_final_w):
    p = _prepare(norm_ffn1_w[0], ffn1_w_gate[0], ffn1_w_up[0], ffn1_w_down[0], norm_mix_w[0], w_in[0],
                 conv_w[0], conv_b[0], dt_bias_fwd[0], dt_bias_bwd[0], a_log_fwd[0], a_log_bwd[0],
                 d_skip[0], ssm_norm_w[0], attn_sink[0], w_out[0], norm_ffn2_w[0], ffn2_w_gate[0],
                 ffn2_w_up[0], ffn2_w_down[0], norm_final_w)
    return (_trunk(x_prompt, p), _trunk(x_sample, p))
```

```python
import functools
import math

import jax
import jax.numpy as jnp
import numpy as np
from jax import lax
from jax.experimental import pallas as pl
from jax.experimental.pallas import tpu as pltpu

F32 = jnp.float32
BF16 = jnp.bfloat16

D_MODEL = 1024
D_FF = 2816
SSM_HEADS = 16
SSM_HEAD_DIM = 64
D_SSM = SSM_HEADS * SSM_HEAD_DIM
SSM_GROUPS = 2
D_STATE = 128
CONV_K = 5
CONV_DIM = D_SSM + 2 * SSM_GROUPS * D_STATE
N_HEADS = 16
KV_HEADS = 4
HEAD_DIM = 64
D_ATTN = N_HEADS * HEAD_DIM
D_KV = KV_HEADS * HEAD_DIM
WINDOW = 128
EPS = 1e-6
LOG2E = math.log2(math.e)
Q_SCALE = LOG2E / math.sqrt(HEAD_DIM)

LANES = 128
SUBLANES = 8
VMEM_BYTES_V7X = 64 * 1024 * 1024

CHUNK = LANES
TOK_TILE = 512
SEQ_TILE = 512
CHUNKS_PER_STEP = SEQ_TILE // CHUNK
MIX_TILE = 256
MIX_CHUNKS = MIX_TILE // CHUNK
MXU_TILE = 256
FF_SPLIT = (D_FF // MXU_TILE + 1) // 2 * MXU_TILE
FF_CHUNK = FF_SPLIT
HALO = 2 * SUBLANES
CONV_PIECE = MXU_TILE
DT_PAD = LANES
HEADS_PER_GROUP = SSM_HEADS // SSM_GROUPS
GROUP_W = HEADS_PER_GROUP * SSM_HEAD_DIM

ATTN_HEAD_ORDER = (0, 4, 1, 5, 2, 6, 3, 7, 8, 12, 9, 13, 10, 14, 11, 15)


def _vmem_limit(resident_bytes):
    return int(min(VMEM_BYTES_V7X - 8 * 1024 * 1024, 2 * resident_bytes + 8 * 1024 * 1024))


def _resident(shape):
    nd = len(shape)
    return pl.BlockSpec(shape, lambda *_: (0,) * nd, pipeline_mode=pl.Buffered(1))


def _dot(a, b):
    return jnp.dot(a, b, preferred_element_type=F32)


def _dot_nt(a, b):
    return lax.dot_general(a, b, (((1,), (1,)), ((), ())), preferred_element_type=F32)


def _dot_tn(a, b):
    return lax.dot_general(a, b, (((0,), (0,)), ((), ())), preferred_element_type=F32)


def _rmsnorm(x, w):
    return x * lax.rsqrt(jnp.mean(x * x, axis=-1, keepdims=True) + EPS) * w


def _silu(x):
    return x * jax.nn.sigmoid(x)


def _softplus(x):
    return jnp.maximum(x, 0.0) + jnp.log1p(jnp.exp(-jnp.abs(x)))


def _split2(x):
    hi = x.astype(BF16)
    lo = (x - hi.astype(F32)).astype(BF16)
    return hi, lo


def _split3(x):
    hi = x.astype(BF16)
    r = x - hi.astype(F32)
    mid = r.astype(BF16)
    lo = (r - mid.astype(F32)).astype(BF16)
    return hi, mid, lo


def _swiglu(xn, wg_ref, wu_ref, wd_ref):
    acc = None
    for cols in (slice(0, FF_SPLIT), slice(FF_SPLIT, D_FF)):
        g = _dot(xn, wg_ref[:, cols])
        u = _dot(xn, wu_ref[:, cols])
        part = _dot((_silu(g) * u).astype(BF16), wd_ref[cols, :])
        acc = part if acc is None else acc + part
    return acc


def _ffn1_kernel(x_ref, nw_ref, wg_ref, wu_ref, wd_ref, o_ref):
    x = x_ref[...]
    xn = _rmsnorm(x, nw_ref[...]).astype(BF16)
    o_ref[...] = x + 0.5 * _swiglu(xn, wg_ref, wu_ref, wd_ref)


def _ffn1(x2d, nw, wg, wu, wd):
    n = x2d.shape[0]
    tile = pl.BlockSpec((TOK_TILE, D_MODEL), lambda i: (i, 0))
    resident = 2 * 3 * D_MODEL * D_FF + 4 * 4 * TOK_TILE * D_MODEL + 3 * 4 * TOK_TILE * FF_CHUNK
    return pl.pallas_call(
        _ffn1_kernel,
        grid=(n // TOK_TILE,),
        in_specs=[tile, _resident((1, D_MODEL)), _resident((D_MODEL, D_FF)),
                  _resident((D_MODEL, D_FF)), _resident((D_FF, D_MODEL))],
        out_specs=tile,
        out_shape=jax.ShapeDtypeStruct((n, D_MODEL), F32),
        compiler_params=pltpu.CompilerParams(
            dimension_semantics=("parallel",), vmem_limit_bytes=_vmem_limit(resident)),
        name="ffn1",
    )(x2d, nw, wg, wu, wd)


def _tri(lower):
    r = lax.broadcasted_iota(jnp.int32, (CHUNK, CHUNK), 0)
    c = lax.broadcasted_iota(jnp.int32, (CHUNK, CHUNK), 1)
    return jnp.where((r >= c) if lower else (r <= c), 1.0, 0.0).astype(BF16)


def _tri_matmul(tri, x):
    hi, mid, lo = _split3(x)
    out = _dot(tri, jnp.concatenate([hi, mid, lo], axis=1))
    w = x.shape[1]
    return out[:, 0:w] + out[:, w:2 * w] + out[:, 2 * w:3 * w]


def _neg_exp_row(a_log_ref):
    lane = lax.broadcasted_iota(jnp.int32, (1, DT_PAD), 1)
    return jnp.where(lane < 2 * SSM_HEADS, -jnp.exp(a_log_ref[...]), 0.0)


def _expand_heads(x, e2_ref):
    hi, lo = _split2(x)
    return _dot(jnp.concatenate([hi, lo], axis=1), e2_ref[...])


def _mix_in_kernel(xm_ref, xp_ref, xn_ref, nw_ref, wz_ref, wx_ref, wdt_ref, wq_ref, wk_ref, wv_ref,
                   cw_ref, cb_ref, bias_ref, a_ref, e2b_ref,
                   z_ref, u_ref, dt_ref, q_ref, k_ref, v_ref, prev_ref, pad_ref, st_ref):
    j = pl.program_id(1)
    last = pl.num_programs(1) - 1

    @pl.when(j == 0)
    def _():
        st_ref[...] = jnp.zeros_like(st_ref)

    h = _rmsnorm(xm_ref[...], nw_ref[...]).astype(BF16)
    h_wide = jnp.concatenate([_rmsnorm(xp_ref[...], nw_ref[...]).astype(BF16), h,
                              _rmsnorm(xn_ref[...], nw_ref[...]).astype(BF16)], axis=0)
    dt = _dot(h, wdt_ref[...])
    dt_ref[...] = dt

    def project(w_ref, o_ref, cols, scale=None):
        def run():
            y = _dot(h, w_ref[:, cols])
            o_ref[:, cols] = (y if scale is None else y * scale).astype(BF16)
        return run

    tiles = lambda width: [slice(t * MXU_TILE, (t + 1) * MXU_TILE) for t in range(width // MXU_TILE)]
    fillers = ([project(wz_ref, z_ref, cols) for cols in tiles(D_SSM)]
               + [project(wq_ref, q_ref, cols, Q_SCALE) for cols in tiles(D_ATTN)]
               + [project(wk_ref, k_ref, cols) for cols in tiles(D_KV)]
               + [project(wv_ref, v_ref, cols) for cols in tiles(D_KV)])

    def fill(count=1):
        for _ in range(count):
            if fillers:
                fillers.pop(0)()

    first_tap = HALO - (CONV_K - 1) // 2
    n_pieces = CONV_DIM // CONV_PIECE

    def project_conv_input(piece):
        cols = slice(piece * CONV_PIECE, (piece + 1) * CONV_PIECE)
        xbc = _dot(h_wide, wx_ref[:, cols])
        pad_ref[0:HALO, cols] = jnp.where(j < last, xbc[0:HALO, :], 0.0)
        pad_ref[HALO:HALO + SEQ_TILE, cols] = xbc[HALO:HALO + SEQ_TILE, :]
        pad_ref[HALO + SEQ_TILE:2 * HALO + SEQ_TILE, cols] = jnp.where(
            j > 0, xbc[HALO + SEQ_TILE:2 * HALO + SEQ_TILE, :], 0.0)

    def conv(piece):
        cols = slice(piece * CONV_PIECE, (piece + 1) * CONV_PIECE)
        for c in range(CHUNKS_PER_STEP):
            acc = cb_ref[:, cols]
            for k in range(CONV_K):
                start = c * CHUNK + first_tap + k
                acc = acc + pad_ref[start:start + CHUNK, cols] * cw_ref[k:k + 1, cols]
            u_ref[c * CHUNK:(c + 1) * CHUNK, cols] = _silu(acc).astype(BF16)

    project_conv_input(0)
    for piece in range(n_pieces):
        if piece + 1 < n_pieces:
            project_conv_input(piece + 1)
        conv(piece)
        fill()

    tri_u = _tri(lower=False)
    a_row = _neg_exp_row(a_ref)
    chunk_rows = [slice(c * CHUNK, (c + 1) * CHUNK) for c in range(CHUNKS_PER_STEP)]
    dtvs = [_softplus(dt[r, :] + bias_ref[...]) for r in chunk_rows]
    rcss = [_tri_matmul(tri_u, dtv * a_row) for dtv in dtvs]
    fill()
    w_exps = [_expand_heads(jnp.exp(rcs[0:1, :] - rcs) * dtv, e2b_ref) for rcs, dtv in zip(rcss, dtvs)]
    decs = [_expand_heads(jnp.broadcast_to(jnp.exp(rcs[0:1, :]), (SUBLANES, LANES)), e2b_ref)[0:1, :]
            for rcs in rcss]
    fill()
    contribs = []
    for r, w_exp in zip(chunk_rows, w_exps):
        xsw = (u_ref[r, 0:D_SSM].astype(F32) * w_exp).astype(BF16)
        contribs.append([_dot_tn(u_ref[r, D_SSM + g * D_STATE:D_SSM + (g + 1) * D_STATE],
                                 xsw[:, g * GROUP_W:(g + 1) * GROUP_W]) for g in range(SSM_GROUPS)])
        fill()
    fill(len(fillers))
    for c in reversed(range(CHUNKS_PER_STEP)):
        prev_ref[c] = st_ref[...].astype(BF16)
        for g in range(SSM_GROUPS):
            cols = slice(g * GROUP_W, (g + 1) * GROUP_W)
            st_ref[:, cols] = st_ref[:, cols] * decs[c][:, cols] + contribs[c][g]


def _mix_in(x1, nw, wz, wx, wdt, wq, wk, wv, cw, cb, bias, a_log, e2b):
    b, s, _ = x1.shape
    steps = s // SEQ_TILE
    halo_per_tile = SEQ_TILE // HALO
    n_halo = s // HALO
    widths = (D_SSM, CONV_DIM, DT_PAD, D_ATTN, D_KV, D_KV)
    dtypes = (BF16, BF16, F32, BF16, BF16, BF16)
    rev = lambda bi, j: (bi, steps - 1 - j, 0)
    prv = lambda bi, j: (bi, jnp.maximum((steps - 1 - j) * halo_per_tile - 1, 0), 0)
    nxt = lambda bi, j: (bi, jnp.minimum((steps - j) * halo_per_tile, n_halo - 1), 0)
    resident = (2 * D_MODEL * sum(widths) + 2 * 2 * LANES * D_SSM
                + 2 * 4 * SEQ_TILE * (D_MODEL + sum(widths)) + 2 * 2 * SEQ_TILE * D_SSM
                + 4 * (SEQ_TILE + 2 * HALO) * CONV_DIM + 4 * D_STATE * D_SSM)
    return pl.pallas_call(
        _mix_in_kernel,
        grid=(b, steps),
        in_specs=[pl.BlockSpec((None, SEQ_TILE, D_MODEL), rev),
                  pl.BlockSpec((None, HALO, D_MODEL), prv),
                  pl.BlockSpec((None, HALO, D_MODEL), nxt),
                  _resident((1, D_MODEL))] + [_resident((D_MODEL, w)) for w in widths]
                 + [_resident((SUBLANES, CONV_DIM)), _resident((1, CONV_DIM)),
                    _resident((1, DT_PAD)), _resident((1, DT_PAD)), _resident((2 * LANES, D_SSM))],
        out_specs=[pl.BlockSpec((None, SEQ_TILE, w), rev) for w in widths]
                  + [pl.BlockSpec((None, CHUNKS_PER_STEP, D_STATE, D_SSM),
                                  lambda bi, j: (bi, steps - 1 - j, 0, 0))],
        out_shape=[jax.ShapeDtypeStruct((b, s, w), d) for w, d in zip(widths, dtypes)]
                  + [jax.ShapeDtypeStruct((b, s // CHUNK, D_STATE, D_SSM), BF16)],
        scratch_shapes=[pltpu.VMEM((SEQ_TILE + 2 * HALO, CONV_DIM), F32),
                        pltpu.VMEM((D_STATE, D_SSM), F32)],
        compiler_params=pltpu.CompilerParams(
            dimension_semantics=("arbitrary", "arbitrary"), vmem_limit_bytes=_vmem_limit(resident)),
        name="mix_in",
    )(x1, x1, x1, nw, wz, wx, wdt, wq, wk, wv, cw, cb, bias, a_log, e2b)


def _ssd_prologue(c, u_ref, dt_ref, bias_ref, a_row, tri_both, col):
    rows = slice(c * CHUNK, (c + 1) * CHUNK)
    dtv = _softplus(dt_ref[rows, :] + bias_ref[...])
    both = _tri_matmul(tri_both, dtv * a_row)
    cs = both[0:CHUNK, :]
    rcs = both[CHUNK:2 * CHUNK, :]
    comb = jnp.where(col < SSM_HEADS, cs, rcs)
    comb_t = comb.T
    dt_t = dtv.T

    xs = u_ref[rows, 0:D_SSM].astype(F32)
    bmat = [u_ref[rows, D_SSM + g * D_STATE:D_SSM + (g + 1) * D_STATE] for g in range(SSM_GROUPS)]
    cmat = [u_ref[rows, D_SSM + (SSM_GROUPS + g) * D_STATE:D_SSM + (SSM_GROUPS + g + 1) * D_STATE]
            for g in range(SSM_GROUPS)]
    cb = [_dot_nt(cmat[g], bmat[g]) for g in range(SSM_GROUPS)]
    return dict(dtv=dtv, comb=comb, comb_t=comb_t, dt_t=dt_t, xs=xs, bmat=bmat, cmat=cmat, cb=cb)


def _ssd_head_pair(pro, j, row, col):
    half = SSM_HEAD_DIM
    comb, comb_t, dt_t, dtv = pro["comb"], pro["comb_t"], pro["dt_t"], pro["dtv"]
    g = (2 * j) // HEADS_PER_GROUP
    lane_bcast = lambda a, i: jnp.broadcast_to(a[:, i:i + 1], (CHUNK, CHUNK))
    ms, wf, ef, eb, decf = [], [], [], [], []
    for h in (2 * j, 2 * j + 1):
        csf = lane_bcast(comb, h)
        rcsb = lane_bcast(comb, SSM_HEADS + h)
        df = csf - comb_t[h:h + 1, :]
        db = rcsb - comb_t[SSM_HEADS + h:SSM_HEADS + h + 1, :]
        dtf = dt_t[h:h + 1, :]
        dtb = dt_t[SSM_HEADS + h:SSM_HEADS + h + 1, :]
        dsel = jnp.where(row > col, dtf, jnp.where(row < col, dtb, dtf + dtb))
        decay = jnp.exp(jnp.where(row >= col, df, db))
        ms.append((pro["cb"][g] * decay * dsel).astype(BF16))
        last = csf[CHUNK - 1:CHUNK, :]
        wf.append(jnp.exp(last - csf) * lane_bcast(dtv, h))
        ef.append(jnp.exp(csf))
        eb.append(jnp.exp(rcsb))
        decf.append(jnp.exp(last))
    pair = lambda ab: jnp.where(col[0:ab[0].shape[0], :] < half, ab[0], ab[1])
    xt = pro["xs"][:, j * LANES:(j + 1) * LANES]
    rhs = jnp.concatenate([jnp.where(col < half, xt, 0.0), jnp.where(col >= half, xt, 0.0)],
                          axis=0).astype(BF16)
    ydiag = _dot(jnp.concatenate(ms, axis=1), rhs)
    return dict(ydiag=ydiag, wf=pair(wf), ef=pair(ef), eb=pair(eb), decf=pair(decf))


def _ssd_epilogue(c, pro, pairs, z_ref, prev_ref, dskip_ref, nw_ref, y_ref, st_ref):
    rows = slice(c * CHUNK, (c + 1) * CHUNK)
    xs, bmat, cmat = (pro[n] for n in ("xs", "bmat", "cmat"))
    ydiag, wf, ef, eb, decf = (jnp.concatenate([p[n] for p in pairs], axis=1)
                               for n in ("ydiag", "wf", "ef", "eb", "decf"))
    xsw = (xs * wf).astype(BF16)

    for g in range(SSM_GROUPS):
        cols = slice(g * GROUP_W, (g + 1) * GROUP_W)
        state = st_ref[:, cols]
        yoff_f = _dot(cmat[g], state.astype(BF16))
        yoff_b = _dot(cmat[g], prev_ref[c, :, cols])
        y = (ydiag[:, cols] + ef[:, cols] * yoff_f + eb[:, cols] * yoff_b
             + xs[:, cols] * dskip_ref[:, cols])
        y = y * _silu(z_ref[rows, cols].astype(F32))
        y = y * lax.rsqrt(jnp.mean(y * y, axis=-1, keepdims=True) + EPS) * nw_ref[:, cols]
        y_ref[rows, cols] = y.astype(BF16)
        st_ref[:, cols] = state * decf[:, cols] + _dot_tn(bmat[g], xsw[:, cols])


ATTN_KEYS = 3 * CHUNK
Q_TILES_PER_KV_TILE = (D_ATTN // LANES) // (D_KV // LANES)


def _attn_prologue(c, n_chunks, blk, last_blk, k_refs, v_refs):
    keys = ATTN_KEYS
    half = HEAD_DIM
    rows = slice(c * CHUNK, (c + 1) * CHUNK)

    srow = lax.broadcasted_iota(jnp.int32, (keys, CHUNK), 0)
    tcol = lax.broadcasted_iota(jnp.int32, (keys, CHUNK), 1)
    idist = jnp.abs(tcol + CHUNK - srow)
    lo_key = jnp.where(blk == 0, CHUNK, 0)
    hi_key = jnp.where(blk == last_blk, 2 * CHUNK, keys)
    valid = (idist <= WINDOW) & (srow >= lo_key) & (srow < hi_key)
    neg_dist = jnp.where(valid, -idist.astype(F32), -jnp.inf)
    klane = lax.broadcasted_iota(jnp.int32, (keys, LANES), 1)
    vrow = lax.broadcasted_iota(jnp.int32, (LANES, keys), 0)
    srow16 = lax.broadcasted_iota(jnp.int32, (2 * SUBLANES, 2 * keys), 0)
    scol16 = lax.broadcasted_iota(jnp.int32, (2 * SUBLANES, 2 * keys), 1)
    sum_rows = jnp.where((srow16 < SUBLANES) == (scol16 < keys), 1.0, 0.0).astype(BF16)

    def window(refs, lanes):
        main_ref, prev_ref, next_ref = refs
        prev = prev_ref[:, lanes] if c == 0 else main_ref[(c - 1) * CHUNK:c * CHUNK, lanes]
        nxt = (next_ref[:, lanes] if c == n_chunks - 1
               else main_ref[(c + 1) * CHUNK:(c + 2) * CHUNK, lanes])
        return jnp.concatenate([prev, main_ref[rows, lanes], nxt], axis=0)

    kk, vv = [], []
    for m in range(D_KV // LANES):
        lanes = slice(m * LANES, (m + 1) * LANES)
        kf = window(k_refs, lanes).astype(F32)
        kk.append(jnp.concatenate([jnp.where(klane < half, kf, 0.0), jnp.where(klane >= half, kf, 0.0)],
                                  axis=0).astype(BF16))
        vt_t = window(v_refs, lanes).astype(F32).T
        vm = jnp.concatenate([jnp.where(vrow < half, vt_t, 0.0), jnp.where(vrow >= half, vt_t, 0.0)],
                             axis=1).astype(BF16)
        vv.append(jnp.concatenate([vm, sum_rows], axis=0))
    return dict(neg_dist=neg_dist, kk=kk, vv=vv)


Q_TILES_PER_CALL = MXU_TILE // LANES
assert Q_TILES_PER_KV_TILE % Q_TILES_PER_CALL == 0


def _attn_scores(c, apro, i, q_ref):
    rows = slice(c * CHUNK, (c + 1) * CHUNK)
    kv_tile = i * Q_TILES_PER_CALL // Q_TILES_PER_KV_TILE
    q_stack = jnp.concatenate([q_ref[rows, j * LANES:(j + 1) * LANES]
                               for j in range(Q_TILES_PER_CALL * i, Q_TILES_PER_CALL * (i + 1))], axis=0)
    return _dot_nt(apro["kk"][kv_tile], q_stack)


def _attn_finish(c, apro, i, st, slope_ref, sink_ref, o_ref):
    keys = ATTN_KEYS
    half = HEAD_DIM
    rows = slice(c * CHUNK, (c + 1) * CHUNK)
    kv_tile = i * Q_TILES_PER_CALL // Q_TILES_PER_KV_TILE
    orow = lax.broadcasted_iota(jnp.int32, (LANES, CHUNK), 0)
    ps, mxs = [], []
    for t in range(Q_TILES_PER_CALL):
        p_tile = []
        for e in range(2):
            slot = 2 * (Q_TILES_PER_CALL * i + t) + e
            s = st[e * keys:(e + 1) * keys, t * CHUNK:(t + 1) * CHUNK] + slope_ref[slot] * apro["neg_dist"]
            mx = jnp.maximum(jnp.max(s, axis=0, keepdims=True), sink_ref[slot] * LOG2E)
            p_tile.append(jnp.exp2(s - mx).astype(BF16))
            mxs.append(mx)
        ps.append(jnp.concatenate(p_tile, axis=0))
    ot = _dot(apro["vv"][kv_tile], jnp.concatenate(ps, axis=1))
    for t in range(Q_TILES_PER_CALL):
        j = Q_TILES_PER_CALL * i + t
        ot_t = ot[:, t * CHUNK:(t + 1) * CHUNK]
        inv = [1.0 / (ot_t[LANES + e * SUBLANES:LANES + e * SUBLANES + 1, :]
                      + jnp.exp2(sink_ref[2 * j + e] * LOG2E - mxs[2 * t + e])) for e in range(2)]
        out = ot_t[0:LANES, :] * jnp.where(orow < half, inv[0], inv[1])
        o_ref[rows, j * LANES:(j + 1) * LANES] = out.T.astype(BF16)


def _out_ffn2_slices(ys, ya, x_ref, wos_ref, woa_ref, nw2_ref, wg_ref, wu_ref, wd_ref, nwf_ref, o_ref,
                     x2_ref, xn_ref, acc_ref):
    v = {}
    tile = lambda t: slice(t * MXU_TILE, (t + 1) * MXU_TILE)

    def out_proj(y_ref, w_ref, t, first):
        def run():
            base = x_ref[...] if first else x2_ref[...]
            x2_ref[...] = base + _dot(y_ref[:, tile(t)], w_ref[tile(t), :])
        return run

    def gate_up(t):
        def run():
            if t == 0:
                xn_ref[...] = _rmsnorm(x2_ref[...], nw2_ref[...]).astype(BF16)
            v["g", t] = _dot(xn_ref[...], wg_ref[:, tile(t)])
            v["u", t] = _dot(xn_ref[...], wu_ref[:, tile(t)])
        return run

    def down(t):
        def run():
            hid = (_silu(v.pop(("g", t))) * v.pop(("u", t))).astype(BF16)
            part = _dot(hid, wd_ref[tile(t), :])
            acc_ref[...] = part if t == 0 else acc_ref[...] + part
        return run

    def finish():
        o_ref[...] = _rmsnorm(x2_ref[...] + 0.5 * acc_ref[...], nwf_ref[...])

    n_y = D_SSM // MXU_TILE
    n_ff = D_FF // MXU_TILE
    items = ([out_proj(ys, wos_ref, t, t == 0) for t in range(n_y)]
             + [out_proj(ya, woa_ref, t, False) for t in range(D_ATTN // MXU_TILE)])
    for t in range(n_ff):
        items.append(gate_up(t))
        if t > 0:
            items.append(down(t - 1))
    items += [down(n_ff - 1), finish]
    return items


def _mix_out_kernel(slope_ref, sink_ref, u_ref, dt_ref, z_ref, prev_ref, q_ref,
                    km_ref, kp_ref, kn_ref, vm_ref, vp_ref, vn_ref, x_ref,
                    bias_ref, a_ref, dskip_ref, nw_ref,
                    wos_ref, woa_ref, nw2_ref, wg_ref, wu_ref, wd_ref, nwf_ref,
                    o_ref, ys_ref, ya_ref, st_ref, yps_ref, ypa_ref, x2_ref, xn_ref, acc_ref):
    j = pl.program_id(1)
    n_tiles = pl.num_programs(1) - 1

    @pl.when(j == 0)
    def _():
        st_ref[...] = jnp.zeros_like(st_ref)

    def output_stage_slices():
        yps_ref[...] = ys_ref[...]
        ypa_ref[...] = ya_ref[...]
        return _out_ffn2_slices(yps_ref, ypa_ref, x_ref, wos_ref, woa_ref, nw2_ref,
                                wg_ref, wu_ref, wd_ref, nwf_ref, o_ref, x2_ref, xn_ref, acc_ref)

    def mixers(fillers):
        def fill(count=1):
            for _ in range(count):
                if fillers:
                    fillers.pop(0)()

        tri_both = jnp.concatenate([_tri(lower=True), _tri(lower=False)], axis=0)
        row = lax.broadcasted_iota(jnp.int32, (CHUNK, CHUNK), 0)
        col = lax.broadcasted_iota(jnp.int32, (CHUNK, CHUNK), 1)
        a_row = _neg_exp_row(a_ref)
        n_pairs = SSM_HEADS // 2
        assert n_pairs == D_ATTN // LANES
        fill(2)
        for c in range(MIX_CHUNKS):
            pro = _ssd_prologue(c, u_ref, dt_ref, bias_ref, a_row, tri_both, col)
            fill()
            apro = _attn_prologue(c, MIX_CHUNKS, j * MIX_CHUNKS + c, n_tiles * MIX_CHUNKS - 1,
                                  (km_ref, kp_ref, kn_ref), (vm_ref, vp_ref, vn_ref))
            fill()
            ydiag = []
            n_calls = n_pairs // Q_TILES_PER_CALL
            st_next = _attn_scores(c, apro, 0, q_ref)
            for i in range(n_calls):
                st = st_next
                if i + 1 < n_calls:
                    st_next = _attn_scores(c, apro, i + 1, q_ref)
                fill()
                for t in range(Q_TILES_PER_CALL):
                    ydiag.append(_ssd_head_pair(pro, Q_TILES_PER_CALL * i + t, row, col))
                    fill()
                _attn_finish(c, apro, i, st, slope_ref, sink_ref, ya_ref)
            fill(1 if c + 1 < MIX_CHUNKS else len(fillers))
            _ssd_epilogue(c, pro, ydiag, z_ref, prev_ref, dskip_ref, nw_ref, ys_ref, st_ref)
        assert not fillers

    @pl.when(j == 0)
    def _():
        mixers([])

    @pl.when(jnp.logical_and(j > 0, j < n_tiles))
    def _():
        mixers(output_stage_slices())

    @pl.when(j == n_tiles)
    def _():
        for run in output_stage_slices():
            run()


def _mix_out(u, dt, z, prevb, q, k, v, x1, slopes2, sink, bias, a_log, dskip, nw,
             wos, woa, nw2, wg, wu, wd, nwf):
    b, s, _ = u.shape
    n_tiles = s // MIX_TILE
    nblk = s // CHUNK
    cur = lambda j: jnp.minimum(j, n_tiles - 1)
    mix = lambda bi, j: (bi, cur(j), 0)
    lag = lambda bi, j: (bi, jnp.maximum(j - 1, 0), 0)
    prv = lambda bi, j: (bi, jnp.maximum(cur(j) * MIX_CHUNKS - 1, 0), 0)
    nxt = lambda bi, j: (bi, jnp.minimum((cur(j) + 1) * MIX_CHUNKS, nblk - 1), 0)
    smem = pl.BlockSpec(memory_space=pltpu.SMEM)
    kv_main = pl.BlockSpec((None, MIX_TILE, D_KV), mix)
    kv_prev = pl.BlockSpec((None, CHUNK, D_KV), prv)
    kv_next = pl.BlockSpec((None, CHUNK, D_KV), nxt)
    resident = (2 * (3 * D_MODEL * D_FF + (D_SSM + D_ATTN) * D_MODEL)
                + 2 * 2 * MIX_TILE * (CONV_DIM + 2 * D_SSM + D_ATTN + 2 * D_KV) + 2 * 4 * MIX_TILE * DT_PAD
                + 2 * 2 * 4 * MIX_TILE * D_MODEL + 2 * MIX_TILE * (D_SSM + D_ATTN) + 4 * D_STATE * D_SSM)
    return pl.pallas_call(
        _mix_out_kernel,
        grid=(b, n_tiles + 1),
        in_specs=[smem, smem,
                  pl.BlockSpec((None, MIX_TILE, CONV_DIM), mix),
                  pl.BlockSpec((None, MIX_TILE, DT_PAD), mix),
                  pl.BlockSpec((None, MIX_TILE, D_SSM), mix),
                  pl.BlockSpec((None, MIX_CHUNKS, D_STATE, D_SSM), lambda bi, j: (bi, cur(j), 0, 0)),
                  pl.BlockSpec((None, MIX_TILE, D_ATTN), mix),
                  kv_main, kv_prev, kv_next, kv_main, kv_prev, kv_next,
                  pl.BlockSpec((None, MIX_TILE, D_MODEL), lag),
                  _resident((1, DT_PAD)), _resident((1, DT_PAD)),
                  _resident((1, D_SSM)), _resident((1, D_SSM)),
                  _resident((D_SSM, D_MODEL)), _resident((D_ATTN, D_MODEL)), _resident((1, D_MODEL)),
                  _resident((D_MODEL, D_FF)), _resident((D_MODEL, D_FF)), _resident((D_FF, D_MODEL)),
                  _resident((1, D_MODEL))],
        out_specs=pl.BlockSpec((None, MIX_TILE, D_MODEL), lag),
        out_shape=jax.ShapeDtypeStruct((b, s, D_MODEL), F32),
        scratch_shapes=[pltpu.VMEM((MIX_TILE, D_SSM), BF16), pltpu.VMEM((MIX_TILE, D_ATTN), BF16),
                        pltpu.VMEM((D_STATE, D_SSM), F32),
                        pltpu.VMEM((MIX_TILE, D_SSM), BF16), pltpu.VMEM((MIX_TILE, D_ATTN), BF16),
                        pltpu.VMEM((MIX_TILE, D_MODEL), F32), pltpu.VMEM((MIX_TILE, D_MODEL), BF16),
                        pltpu.VMEM((MIX_TILE, D_MODEL), F32)],
        compiler_params=pltpu.CompilerParams(
            dimension_semantics=("arbitrary", "arbitrary"), vmem_limit_bytes=_vmem_limit(resident)),
        name="mix_out",
    )(slopes2, sink, u, dt, z, prevb, q, k, k, k, v, v, v, x1, bias, a_log, dskip, nw,
      wos, woa, nw2, wg, wu, wd, nwf)


def _head_selection(offset):
    e = np.zeros((2 * LANES, D_SSM), np.float32)
    for h in range(SSM_HEADS):
        e[offset + h, h * SSM_HEAD_DIM:(h + 1) * SSM_HEAD_DIM] = 1.0
        e[LANES + offset + h, h * SSM_HEAD_DIM:(h + 1) * SSM_HEAD_DIM] = 1.0
    return jnp.asarray(e, BF16)


def _prepare(norm_ffn1_w, ffn1_w_gate, ffn1_w_up, ffn1_w_down, norm_mix_w, w_in, conv_w, conv_b,
             dt_bias_fwd, dt_bias_bwd, a_log_fwd, a_log_bwd, d_skip, ssm_norm_w, attn_sink, w_out,
             norm_ffn2_w, ffn2_w_gate, ffn2_w_up, ffn2_w_down, norm_final_w):
    row = lambda v: v.reshape(1, -1).astype(F32)
    splits = np.cumsum((D_SSM, CONV_DIM, SSM_HEADS, SSM_HEADS, D_ATTN, D_KV))
    wz, wx, wdtf, wdtb, wq, wk, wv = jnp.split(w_in, splits, axis=1)
    wdt = jnp.concatenate([wdtf, wdtb, jnp.zeros((D_MODEL, DT_PAD - 2 * SSM_HEADS), F32)], axis=1)
    order = np.asarray(ATTN_HEAD_ORDER)
    cols = (order[:, None] * HEAD_DIM + np.arange(HEAD_DIM)[None, :]).reshape(-1)
    wq = wq[:, cols]
    pad_dt = jnp.zeros((DT_PAD - 2 * SSM_HEADS,), F32)
    slopes = jnp.exp2(-(8.0 / N_HEADS) * jnp.arange(1, N_HEADS + 1, dtype=F32))
    return dict(
        nw1=row(norm_ffn1_w), wg1=ffn1_w_gate.astype(BF16), wu1=ffn1_w_up.astype(BF16),
        wd1=ffn1_w_down.astype(BF16),
        nwm=row(norm_mix_w), wz=wz.astype(BF16), wx=wx.astype(BF16), wdt=wdt.astype(BF16),
        wq=wq.astype(BF16), wk=wk.astype(BF16), wv=wv.astype(BF16),
        cw=jnp.concatenate([conv_w, jnp.zeros((SUBLANES - CONV_K, CONV_DIM), F32)], axis=0),
        cb=row(conv_b),
        dt_bias=row(jnp.concatenate([dt_bias_fwd, dt_bias_bwd, pad_dt])),
        a_log=row(jnp.concatenate([a_log_fwd, a_log_bwd, pad_dt])),
        e2b=_head_selection(SSM_HEADS),
        dskip=row(jnp.repeat(d_skip, SSM_HEAD_DIM)), nws=row(ssm_norm_w),
        slopes2=(slopes * LOG2E)[order], sink=attn_sink.astype(F32)[order],
        wos=w_out[:D_SSM].astype(BF16), woa=w_out[D_SSM:][cols].astype(BF16),
        nw2=row(norm_ffn2_w), wg2=ffn2_w_gate.astype(BF16), wu2=ffn2_w_up.astype(BF16),
        wd2=ffn2_w_down.astype(BF16), nwf=row(norm_final_w),
    )


def _trunk(x, p):
    b, s, _ = x.shape
    assert s % SEQ_TILE == 0 and (b * s) % TOK_TILE == 0 and x.shape[2] == D_MODEL
    n = b * s
    x1 = _ffn1(x.reshape(n, D_MODEL), p["nw1"], p["wg1"], p["wu1"], p["wd1"])
    z, u, dt, q, k, v, prevb = _mix_in(
        x1.reshape(b, s, D_MODEL), p["nwm"], p["wz"], p["wx"], p["wdt"], p["wq"], p["wk"], p["wv"],
        p["cw"], p["cb"], p["dt_bias"], p["a_log"], p["e2b"])
    return _mix_out(u, dt, z, prevb, q, k, v, x1.reshape(b, s, D_MODEL), p["slopes2"], p["sink"],
                    p["dt_bias"], p["a_log"], p["dskip"], p["nws"],
                    p["wos"], p["woa"], p["nw2"], p["wg2"], p["wu2"], p["wd2"], p["nwf"])


def kernel(x_prompt, x_sample, norm_ffn1_w, ffn1_w_gate, ffn1_w_up, ffn1_w_down, norm_mix_w, w_in, conv_w, conv_b, dt_bias_fwd, dt_bias_bwd, a_log_fwd, a_log_bwd, d_skip, ssm_norm_w, attn_sink, w_out, norm_ffn2_w, ffn2_w_gate, ffn2_w_up, ffn2_w_down, norm_final_w):
    p = _prepare(norm_ffn1_w[0], ffn1_w_gate[0], ffn1_w_up[0], ffn1_w_down[0], norm_mix_w[0], w_in[0],
                 conv_w[0], conv_b[0], dt_bias_fwd[0], dt_bias_bwd[0], a_log_fwd[0], a_log_bwd[0],
                 d_skip[0], ssm_norm_w[0], attn_sink[0], w_out[0], norm_ffn2_w[0], ffn2_w_gate[0],
                 ffn2_w_up[0], ffn2_w_down[0], norm_final_w)
    return (_trunk(x_prompt, p), _trunk(x_sample, p))
```

```python
import functools
import math

import jax
import jax.numpy as jnp
import numpy as np
from jax import lax
from jax.experimental import pallas as pl
from jax.experimental.pallas import tpu as pltpu

F32 = jnp.float32
BF16 = jnp.bfloat16

D_MODEL = 1024
D_FF = 2816
SSM_HEADS = 16
SSM_HEAD_DIM = 64
D_SSM = SSM_HEADS * SSM_HEAD_DIM
SSM_GROUPS = 2
D_STATE = 128
CONV_K = 5
CONV_DIM = D_SSM + 2 * SSM_GROUPS * D_STATE
N_HEADS = 16
KV_HEADS = 4
HEAD_DIM = 64
D_ATTN = N_HEADS * HEAD_DIM
D_KV = KV_HEADS * HEAD_DIM
WINDOW = 128
EPS = 1e-6
LOG2E = math.log2(math.e)
Q_SCALE = LOG2E / math.sqrt(HEAD_DIM)

LANES = 128
SUBLANES = 8
VMEM_BYTES_V7X = 64 * 1024 * 1024

CHUNK = LANES
TOK_TILE = 512
SEQ_TILE = 512
CHUNKS_PER_STEP = SEQ_TILE // CHUNK
MIX_TILE = 512
MIX_CHUNKS = MIX_TILE // CHUNK
MXU_TILE = 256
FF_SPLIT = (D_FF // MXU_TILE + 1) // 2 * MXU_TILE
FF_CHUNK = FF_SPLIT
HALO = 2 * SUBLANES
CONV_PIECE = MXU_TILE
DT_PAD = LANES
HEADS_PER_GROUP = SSM_HEADS // SSM_GROUPS
GROUP_W = HEADS_PER_GROUP * SSM_HEAD_DIM

ATTN_HEAD_ORDER = (0, 4, 1, 5, 2, 6, 3, 7, 8, 12, 9, 13, 10, 14, 11, 15)


def _vmem_limit(resident_bytes):
    return int(min(VMEM_BYTES_V7X - 3 * 1024 * 1024, 2 * resident_bytes + 8 * 1024 * 1024))


def _resident(shape):
    nd = len(shape)
    return pl.BlockSpec(shape, lambda *_: (0,) * nd, pipeline_mode=pl.Buffered(1))


def _dot(a, b):
    return jnp.dot(a, b, preferred_element_type=F32)


def _dot_nt(a, b):
    return lax.dot_general(a, b, (((1,), (1,)), ((), ())), preferred_element_type=F32)


def _dot_tn(a, b):
    return lax.dot_general(a, b, (((0,), (0,)), ((), ())), preferred_element_type=F32)


def _rmsnorm(x, w):
    return x * lax.rsqrt(jnp.mean(x * x, axis=-1, keepdims=True) + EPS) * w


def _silu(x):
    return x * jax.nn.sigmoid(x)


def _softplus(x):
    return jnp.maximum(x, 0.0) + jnp.log1p(jnp.exp(-jnp.abs(x)))


def _split2(x):
    hi = x.astype(BF16)
    lo = (x - hi.astype(F32)).astype(BF16)
    return hi, lo


def _split3(x):
    hi = x.astype(BF16)
    r = x - hi.astype(F32)
    mid = r.astype(BF16)
    lo = (r - mid.astype(F32)).astype(BF16)
    return hi, mid, lo


def _swiglu(xn, wg_ref, wu_ref, wd_ref):
    acc = None
    for cols in (slice(0, FF_SPLIT), slice(FF_SPLIT, D_FF)):
        g = _dot(xn, wg_ref[:, cols])
        u = _dot(xn, wu_ref[:, cols])
        part = _dot((_silu(g) * u).astype(BF16), wd_ref[cols, :])
        acc = part if acc is None else acc + part
    return acc


def _ffn1_kernel(x_ref, nw_ref, wg_ref, wu_ref, wd_ref, o_ref):
    x = x_ref[...]
    xn = _rmsnorm(x, nw_ref[...]).astype(BF16)
    o_ref[...] = x + 0.5 * _swiglu(xn, wg_ref, wu_ref, wd_ref)


def _ffn1(x2d, nw, wg, wu, wd):
    n = x2d.shape[0]
    tile = pl.BlockSpec((TOK_TILE, D_MODEL), lambda i: (i, 0))
    resident = 2 * 3 * D_MODEL * D_FF + 4 * 4 * TOK_TILE * D_MODEL + 3 * 4 * TOK_TILE * FF_CHUNK
    return pl.pallas_call(
        _ffn1_kernel,
        grid=(n // TOK_TILE,),
        in_specs=[tile, _resident((1, D_MODEL)), _resident((D_MODEL, D_FF)),
                  _resident((D_MODEL, D_FF)), _resident((D_FF, D_MODEL))],
        out_specs=tile,
        out_shape=jax.ShapeDtypeStruct((n, D_MODEL), F32),
        compiler_params=pltpu.CompilerParams(
            dimension_semantics=("parallel",), vmem_limit_bytes=_vmem_limit(resident)),
        name="ffn1",
    )(x2d, nw, wg, wu, wd)


def _tri(lower):
    r = lax.broadcasted_iota(jnp.int32, (CHUNK, CHUNK), 0)
    c = lax.broadcasted_iota(jnp.int32, (CHUNK, CHUNK), 1)
    return jnp.where((r >= c) if lower else (r <= c), 1.0, 0.0).astype(BF16)


def _tri_matmul(tri, x):
    hi, mid, lo = _split3(x)
    out = _dot(tri, jnp.concatenate([hi, mid, lo], axis=1))
    w = x.shape[1]
    return out[:, 0:w] + out[:, w:2 * w] + out[:, 2 * w:3 * w]


def _neg_exp_row(a_log_ref):
    lane = lax.broadcasted_iota(jnp.int32, (1, DT_PAD), 1)
    return jnp.where(lane < 2 * SSM_HEADS, -jnp.exp(a_log_ref[...]), 0.0)


def _expand_heads(x, e2_ref):
    hi, lo = _split2(x)
    return _dot(jnp.concatenate([hi, lo], axis=1), e2_ref[...])


def _mix_in_kernel(xm_ref, xp_ref, xn_ref, nw_ref, wz_ref, wx_ref, wdt_ref, wq_ref, wk_ref, wv_ref,
                   cw_ref, cb_ref, bias_ref, a_ref, e2b_ref,
                   z_ref, u_ref, dt_ref, q_ref, k_ref, v_ref, prev_ref, pad_ref, st_ref):
    j = pl.program_id(1)
    last = pl.num_programs(1) - 1

    @pl.when(j == 0)
    def _():
        st_ref[...] = jnp.zeros_like(st_ref)

    h = _rmsnorm(xm_ref[...], nw_ref[...]).astype(BF16)
    h_wide = jnp.concatenate([_rmsnorm(xp_ref[...], nw_ref[...]).astype(BF16), h,
                              _rmsnorm(xn_ref[...], nw_ref[...]).astype(BF16)], axis=0)
    dt = _dot(h, wdt_ref[...])
    dt_ref[...] = dt

    def project(w_ref, o_ref, cols, scale=None):
        def run():
            y = _dot(h, w_ref[:, cols])
            o_ref[:, cols] = (y if scale is None else y * scale).astype(BF16)
        return run

    tiles = lambda width: [slice(t * MXU_TILE, (t + 1) * MXU_TILE) for t in range(width // MXU_TILE)]
    fillers = ([project(wz_ref, z_ref, cols) for cols in tiles(D_SSM)]
               + [project(wq_ref, q_ref, cols, Q_SCALE) for cols in tiles(D_ATTN)]
               + [project(wk_ref, k_ref, cols) for cols in tiles(D_KV)]
               + [project(wv_ref, v_ref, cols) for cols in tiles(D_KV)])

    def fill(count=1):
        for _ in range(count):
            if fillers:
                fillers.pop(0)()

    first_tap = HALO - (CONV_K - 1) // 2
    n_pieces = CONV_DIM // CONV_PIECE

    def project_conv_input(piece):
        cols = slice(piece * CONV_PIECE, (piece + 1) * CONV_PIECE)
        xbc = _dot(h_wide, wx_ref[:, cols])
        pad_ref[0:HALO, cols] = jnp.where(j < last, xbc[0:HALO, :], 0.0)
        pad_ref[HALO:HALO + SEQ_TILE, cols] = xbc[HALO:HALO + SEQ_TILE, :]
        pad_ref[HALO + SEQ_TILE:2 * HALO + SEQ_TILE, cols] = jnp.where(
            j > 0, xbc[HALO + SEQ_TILE:2 * HALO + SEQ_TILE, :], 0.0)

    def conv(piece):
        cols = slice(piece * CONV_PIECE, (piece + 1) * CONV_PIECE)
        for c in range(CHUNKS_PER_STEP):
            acc = cb_ref[:, cols]
            for k in range(CONV_K):
                start = c * CHUNK + first_tap + k
                acc = acc + pad_ref[start:start + CHUNK, cols] * cw_ref[k:k + 1, cols]
            u_ref[c * CHUNK:(c + 1) * CHUNK, cols] = _silu(acc).astype(BF16)

    project_conv_input(0)
    for piece in range(n_pieces):
        if piece + 1 < n_pieces:
            project_conv_input(piece + 1)
        conv(piece)
        fill()

    tri_u = _tri(lower=False)
    a_row = _neg_exp_row(a_ref)
    chunk_rows = [slice(c * CHUNK, (c + 1) * CHUNK) for c in range(CHUNKS_PER_STEP)]
    dtvs = [_softplus(dt[r, :] + bias_ref[...]) for r in chunk_rows]
    rcss = [_tri_matmul(tri_u, dtv * a_row) for dtv in dtvs]
    fill()
    w_exps = [_expand_heads(jnp.exp(rcs[0:1, :] - rcs) * dtv, e2b_ref) for rcs, dtv in zip(rcss, dtvs)]
    decs = [_expand_heads(jnp.broadcast_to(jnp.exp(rcs[0:1, :]), (SUBLANES, LANES)), e2b_ref)[0:1, :]
            for rcs in rcss]
    fill()
    contribs = []
    for r, w_exp in zip(chunk_rows, w_exps):
        xsw = (u_ref[r, 0:D_SSM].astype(F32) * w_exp).astype(BF16)
        contribs.append([_dot_tn(u_ref[r, D_SSM + g * D_STATE:D_SSM + (g + 1) * D_STATE],
                                 xsw[:, g * GROUP_W:(g + 1) * GROUP_W]) for g in range(SSM_GROUPS)])
        fill()
    fill(len(fillers))
    for c in reversed(range(CHUNKS_PER_STEP)):
        prev_ref[c] = st_ref[...].astype(BF16)
        for g in range(SSM_GROUPS):
            cols = slice(g * GROUP_W, (g + 1) * GROUP_W)
            st_ref[:, cols] = st_ref[:, cols] * decs[c][:, cols] + contribs[c][g]


def _mix_in(x1, nw, wz, wx, wdt, wq, wk, wv, cw, cb, bias, a_log, e2b):
    b, s, _ = x1.shape
    steps = s // SEQ_TILE
    halo_per_tile = SEQ_TILE // HALO
    n_halo = s // HALO
    widths = (D_SSM, CONV_DIM, DT_PAD, D_ATTN, D_KV, D_KV)
    dtypes = (BF16, BF16, F32, BF16, BF16, BF16)
    rev = lambda bi, j: (bi, steps - 1 - j, 0)
    prv = lambda bi, j: (bi, jnp.maximum((steps - 1 - j) * halo_per_tile - 1, 0), 0)
    nxt = lambda bi, j: (bi, jnp.minimum((steps - j) * halo_per_tile, n_halo - 1), 0)
    resident = (2 * D_MODEL * sum(widths) + 2 * 2 * LANES * D_SSM
                + 2 * 4 * SEQ_TILE * (D_MODEL + sum(widths)) + 2 * 2 * SEQ_TILE * D_SSM
                + 4 * (SEQ_TILE + 2 * HALO) * CONV_DIM + 4 * D_STATE * D_SSM)
    return pl.pallas_call(
        _mix_in_kernel,
        grid=(b, steps),
        in_specs=[pl.BlockSpec((None, SEQ_TILE, D_MODEL), rev),
                  pl.BlockSpec((None, HALO, D_MODEL), prv),
                  pl.BlockSpec((None, HALO, D_MODEL), nxt),
                  _resident((1, D_MODEL))] + [_resident((D_MODEL, w)) for w in widths]
                 + [_resident((SUBLANES, CONV_DIM)), _resident((1, CONV_DIM)),
                    _resident((1, DT_PAD)), _resident((1, DT_PAD)), _resident((2 * LANES, D_SSM))],
        out_specs=[pl.BlockSpec((None, SEQ_TILE, w), rev) for w in widths]
                  + [pl.BlockSpec((None, CHUNKS_PER_STEP, D_STATE, D_SSM),
                                  lambda bi, j: (bi, steps - 1 - j, 0, 0))],
        out_shape=[jax.ShapeDtypeStruct((b, s, w), d) for w, d in zip(widths, dtypes)]
                  + [jax.ShapeDtypeStruct((b, s // CHUNK, D_STATE, D_SSM), BF16)],
        scratch_shapes=[pltpu.VMEM((SEQ_TILE + 2 * HALO, CONV_DIM), F32),
                        pltpu.VMEM((D_STATE, D_SSM), F32)],
        compiler_params=pltpu.CompilerParams(
            dimension_semantics=("arbitrary", "arbitrary"), vmem_limit_bytes=_vmem_limit(resident)),
        name="mix_in",
    )(x1, x1, x1, nw, wz, wx, wdt, wq, wk, wv, cw, cb, bias, a_log, e2b)


def _ssd_prologue(c, u_ref, dt_ref, bias_ref, a_row, tri_both, col):
    rows = slice(c * CHUNK, (c + 1) * CHUNK)
    dtv = _softplus(dt_ref[rows, :] + bias_ref[...])
    both = _tri_matmul(tri_both, dtv * a_row)
    cs = both[0:CHUNK, :]
    rcs = both[CHUNK:2 * CHUNK, :]
    comb = jnp.where(col < SSM_HEADS, cs, rcs)
    comb_t = comb.T
    dt_t = dtv.T

    xs = u_ref[rows, 0:D_SSM].astype(F32)
    bmat = [u_ref[rows, D_SSM + g * D_STATE:D_SSM + (g + 1) * D_STATE] for g in range(SSM_GROUPS)]
    cmat = [u_ref[rows, D_SSM + (SSM_GROUPS + g) * D_STATE:D_SSM + (SSM_GROUPS + g + 1) * D_STATE]
            for g in range(SSM_GROUPS)]
    cb = [_dot_nt(cmat[g], bmat[g]) for g in range(SSM_GROUPS)]
    return dict(dtv=dtv, comb=comb, comb_t=comb_t, dt_t=dt_t, xs=xs, bmat=bmat, cmat=cmat, cb=cb)


def _ssd_head_pair(pro, j, row, col):
    half = SSM_HEAD_DIM
    comb, comb_t, dt_t, dtv = pro["comb"], pro["comb_t"], pro["dt_t"], pro["dtv"]
    g = (2 * j) // HEADS_PER_GROUP
    lane_bcast = lambda a, i: jnp.broadcast_to(a[:, i:i + 1], (CHUNK, CHUNK))
    ms, wf, ef, eb, decf = [], [], [], [], []
    for h in (2 * j, 2 * j + 1):
        csf = lane_bcast(comb, h)
        rcsb = lane_bcast(comb, SSM_HEADS + h)
        df = csf - comb_t[h:h + 1, :]
        db = rcsb - comb_t[SSM_HEADS + h:SSM_HEADS + h + 1, :]
        dtf = dt_t[h:h + 1, :]
        dtb = dt_t[SSM_HEADS + h:SSM_HEADS + h + 1, :]
        dsel = jnp.where(row > col, dtf, jnp.where(row < col, dtb, dtf + dtb))
        decay = jnp.exp(jnp.where(row >= col, df, db))
        ms.append((pro["cb"][g] * decay * dsel).astype(BF16))
        last = csf[CHUNK - 1:CHUNK, :]
        wf.append(jnp.exp(last - csf) * lane_bcast(dtv, h))
        ef.append(jnp.exp(csf))
        eb.append(jnp.exp(rcsb))
        decf.append(jnp.exp(last))
    pair = lambda ab: jnp.where(col[0:ab[0].shape[0], :] < half, ab[0], ab[1])
    xt = pro["xs"][:, j * LANES:(j + 1) * LANES]
    rhs = jnp.concatenate([jnp.where(col < half, xt, 0.0), jnp.where(col >= half, xt, 0.0)],
                          axis=0).astype(BF16)
    ydiag = _dot(jnp.concatenate(ms, axis=1), rhs)
    return dict(ydiag=ydiag, wf=pair(wf), ef=pair(ef), eb=pair(eb), decf=pair(decf))


def _ssd_epilogue(c, pro, pairs, z_ref, prev_ref, dskip_ref, nw_ref, y_ref, st_ref):
    rows = slice(c * CHUNK, (c + 1) * CHUNK)
    xs, bmat, cmat = (pro[n] for n in ("xs", "bmat", "cmat"))
    ydiag, wf, ef, eb, decf = (jnp.concatenate([p[n] for p in pairs], axis=1)
                               for n in ("ydiag", "wf", "ef", "eb", "decf"))
    xsw = (xs * wf).astype(BF16)

    for g in range(SSM_GROUPS):
        cols = slice(g * GROUP_W, (g + 1) * GROUP_W)
        state = st_ref[:, cols]
        yoff_f = _dot(cmat[g], state.astype(BF16))
        yoff_b = _dot(cmat[g], prev_ref[c, :, cols])
        y = (ydiag[:, cols] + ef[:, cols] * yoff_f + eb[:, cols] * yoff_b
             + xs[:, cols] * dskip_ref[:, cols])
        y = y * _silu(z_ref[rows, cols].astype(F32))
        y = y * lax.rsqrt(jnp.mean(y * y, axis=-1, keepdims=True) + EPS) * nw_ref[:, cols]
        y_ref[rows, cols] = y.astype(BF16)
        st_ref[:, cols] = state * decf[:, cols] + _dot_tn(bmat[g], xsw[:, cols])


ATTN_KEYS = 3 * CHUNK
Q_TILES_PER_KV_TILE = (D_ATTN // LANES) // (D_KV // LANES)


def _attn_prologue(c, n_chunks, blk, last_blk, k_refs, v_refs):
    keys = ATTN_KEYS
    half = HEAD_DIM
    rows = slice(c * CHUNK, (c + 1) * CHUNK)

    srow = lax.broadcasted_iota(jnp.int32, (keys, CHUNK), 0)
    tcol = lax.broadcasted_iota(jnp.int32, (keys, CHUNK), 1)
    idist = jnp.abs(tcol + CHUNK - srow)
    lo_key = jnp.where(blk == 0, CHUNK, 0)
    hi_key = jnp.where(blk == last_blk, 2 * CHUNK, keys)
    valid = (idist <= WINDOW) & (srow >= lo_key) & (srow < hi_key)
    neg_dist = jnp.where(valid, -idist.astype(F32), -jnp.inf)
    klane = lax.broadcasted_iota(jnp.int32, (keys, LANES), 1)
    vrow = lax.broadcasted_iota(jnp.int32, (LANES, keys), 0)
    srow16 = lax.broadcasted_iota(jnp.int32, (2 * SUBLANES, 2 * keys), 0)
    scol16 = lax.broadcasted_iota(jnp.int32, (2 * SUBLANES, 2 * keys), 1)
    sum_rows = jnp.where((srow16 < SUBLANES) == (scol16 < keys), 1.0, 0.0).astype(BF16)

    def window(refs, lanes):
        main_ref, prev_ref, next_ref = refs
        prev = prev_ref[:, lanes] if c == 0 else main_ref[(c - 1) * CHUNK:c * CHUNK, lanes]
        nxt = (next_ref[:, lanes] if c == n_chunks - 1
               else main_ref[(c + 1) * CHUNK:(c + 2) * CHUNK, lanes])
        return jnp.concatenate([prev, main_ref[rows, lanes], nxt], axis=0)

    kk, vv = [], []
    for m in range(D_KV // LANES):
        lanes = slice(m * LANES, (m + 1) * LANES)
        kf = window(k_refs, lanes).astype(F32)
        kk.append(jnp.concatenate([jnp.where(klane < half, kf, 0.0), jnp.where(klane >= half, kf, 0.0)],
                                  axis=0).astype(BF16))
        vt_t = window(v_refs, lanes).astype(F32).T
        vm = jnp.concatenate([jnp.where(vrow < half, vt_t, 0.0), jnp.where(vrow >= half, vt_t, 0.0)],
                             axis=1).astype(BF16)
        vv.append(jnp.concatenate([vm, sum_rows], axis=0))
    return dict(neg_dist=neg_dist, kk=kk, vv=vv)


Q_TILES_PER_CALL = MXU_TILE // LANES
assert Q_TILES_PER_KV_TILE % Q_TILES_PER_CALL == 0


def _attn_scores(c, apro, i, q_ref):
    rows = slice(c * CHUNK, (c + 1) * CHUNK)
    kv_tile = i * Q_TILES_PER_CALL // Q_TILES_PER_KV_TILE
    q_stack = jnp.concatenate([q_ref[rows, j * LANES:(j + 1) * LANES]
                               for j in range(Q_TILES_PER_CALL * i, Q_TILES_PER_CALL * (i + 1))], axis=0)
    return _dot_nt(apro["kk"][kv_tile], q_stack)


def _attn_finish(c, apro, i, st, slope_ref, sink_ref, o_ref):
    keys = ATTN_KEYS
    half = HEAD_DIM
    rows = slice(c * CHUNK, (c + 1) * CHUNK)
    kv_tile = i * Q_TILES_PER_CALL // Q_TILES_PER_KV_TILE
    orow = lax.broadcasted_iota(jnp.int32, (LANES, CHUNK), 0)
    ps, mxs = [], []
    for t in range(Q_TILES_PER_CALL):
        p_tile = []
        for e in range(2):
            slot = 2 * (Q_TILES_PER_CALL * i + t) + e
            s = st[e * keys:(e + 1) * keys, t * CHUNK:(t + 1) * CHUNK] + slope_ref[slot] * apro["neg_dist"]
            mx = jnp.maximum(jnp.max(s, axis=0, keepdims=True), sink_ref[slot] * LOG2E)
            p_tile.append(jnp.exp2(s - mx).astype(BF16))
            mxs.append(mx)
        ps.append(jnp.concatenate(p_tile, axis=0))
    ot = _dot(apro["vv"][kv_tile], jnp.concatenate(ps, axis=1))
    for t in range(Q_TILES_PER_CALL):
        j = Q_TILES_PER_CALL * i + t
        ot_t = ot[:, t * CHUNK:(t + 1) * CHUNK]
        inv = [1.0 / (ot_t[LANES + e * SUBLANES:LANES + e * SUBLANES + 1, :]
                      + jnp.exp2(sink_ref[2 * j + e] * LOG2E - mxs[2 * t + e])) for e in range(2)]
        out = ot_t[0:LANES, :] * jnp.where(orow < half, inv[0], inv[1])
        o_ref[rows, j * LANES:(j + 1) * LANES] = out.T.astype(BF16)


def _out_ffn2_slices(ys, ya, x_ref, wos_ref, woa_ref, nw2_ref, wg_ref, wu_ref, wd_ref, nwf_ref, o_ref,
                     x2_ref, xn_ref, acc_ref):
    v = {}
    tile = lambda t: slice(t * MXU_TILE, (t + 1) * MXU_TILE)

    def out_proj(y_ref, w_ref, t, first):
        def run():
            base = x_ref[...] if first else x2_ref[...]
            x2_ref[...] = base + _dot(y_ref[:, tile(t)], w_ref[tile(t), :])
        return run

    def gate_up(t):
        def run():
            if t == 0:
                xn_ref[...] = _rmsnorm(x2_ref[...], nw2_ref[...]).astype(BF16)
            v["g", t] = _dot(xn_ref[...], wg_ref[:, tile(t)])
            v["u", t] = _dot(xn_ref[...], wu_ref[:, tile(t)])
        return run

    def down(t):
        def run():
            hid = (_silu(v.pop(("g", t))) * v.pop(("u", t))).astype(BF16)
            part = _dot(hid, wd_ref[tile(t), :])
            acc_ref[...] = part if t == 0 else acc_ref[...] + part
        return run

    def finish():
        o_ref[...] = _rmsnorm(x2_ref[...] + 0.5 * acc_ref[...], nwf_ref[...])

    n_y = D_SSM // MXU_TILE
    n_ff = D_FF // MXU_TILE
    items = ([out_proj(ys, wos_ref, t, t == 0) for t in range(n_y)]
             + [out_proj(ya, woa_ref, t, False) for t in range(D_ATTN // MXU_TILE)])
    for t in range(n_ff):
        items.append(gate_up(t))
        if t > 0:
            items.append(down(t - 1))
    items += [down(n_ff - 1), finish]
    return items


def _mix_out_kernel(slope_ref, sink_ref, u_ref, dt_ref, z_ref, prev_ref, q_ref,
                    km_ref, kp_ref, kn_ref, vm_ref, vp_ref, vn_ref, x_ref,
                    bias_ref, a_ref, dskip_ref, nw_ref,
                    wos_ref, woa_ref, nw2_ref, wg_ref, wu_ref, wd_ref, nwf_ref,
                    o_ref, ys_ref, ya_ref, st_ref, yps_ref, ypa_ref, x2_ref, xn_ref, acc_ref):
    j = pl.program_id(1)
    n_tiles = pl.num_programs(1) - 1

    @pl.when(j == 0)
    def _():
        st_ref[...] = jnp.zeros_like(st_ref)

    def output_stage_slices():
        yps_ref[...] = ys_ref[...]
        ypa_ref[...] = ya_ref[...]
        return _out_ffn2_slices(yps_ref, ypa_ref, x_ref, wos_ref, woa_ref, nw2_ref,
                                wg_ref, wu_ref, wd_ref, nwf_ref, o_ref, x2_ref, xn_ref, acc_ref)

    def mixers(fillers):
        n_items = len(fillers)
        points_per_chunk = 3 + 3 * (SSM_HEADS // 2 // Q_TILES_PER_CALL)
        n_points = 2 + points_per_chunk * MIX_CHUNKS
        seen = [0]

        def fill(count=1):
            seen[0] += count
            target = -(-n_items * seen[0] // n_points)
            while fillers and n_items - len(fillers) < target:
                fillers.pop(0)()

        tri_both = jnp.concatenate([_tri(lower=True), _tri(lower=False)], axis=0)
        row = lax.broadcasted_iota(jnp.int32, (CHUNK, CHUNK), 0)
        col = lax.broadcasted_iota(jnp.int32, (CHUNK, CHUNK), 1)
        a_row = _neg_exp_row(a_ref)
        n_pairs = SSM_HEADS // 2
        assert n_pairs == D_ATTN // LANES
        fill(2)
        for c in range(MIX_CHUNKS):
            pro = _ssd_prologue(c, u_ref, dt_ref, bias_ref, a_row, tri_both, col)
            fill()
            apro = _attn_prologue(c, MIX_CHUNKS, j * MIX_CHUNKS + c, n_tiles * MIX_CHUNKS - 1,
                                  (km_ref, kp_ref, kn_ref), (vm_ref, vp_ref, vn_ref))
            fill()
            ydiag = []
            n_calls = n_pairs // Q_TILES_PER_CALL
            st_next = _attn_scores(c, apro, 0, q_ref)
            for i in range(n_calls):
                st = st_next
                if i + 1 < n_calls:
                    st_next = _attn_scores(c, apro, i + 1, q_ref)
                fill()
                for t in range(Q_TILES_PER_CALL):
                    ydiag.append(_ssd_head_pair(pro, Q_TILES_PER_CALL * i + t, row, col))
                    fill()
                _attn_finish(c, apro, i, st, slope_ref, sink_ref, ya_ref)
            fill()
            _ssd_epilogue(c, pro, ydiag, z_ref, prev_ref, dskip_ref, nw_ref, ys_ref, st_ref)
        assert not fillers

    @pl.when(j == 0)
    def _():
        mixers([])

    @pl.when(jnp.logical_and(j > 0, j < n_tiles))
    def _():
        mixers(output_stage_slices())

    @pl.when(j == n_tiles)
    def _():
        for run in output_stage_slices():
            run()


def _mix_out(u, dt, z, prevb, q, k, v, x1, slopes2, sink, bias, a_log, dskip, nw,
             wos, woa, nw2, wg, wu, wd, nwf):
    b, s, _ = u.shape
    n_tiles = s // MIX_TILE
    nblk = s // CHUNK
    cur = lambda j: jnp.minimum(j, n_tiles - 1)
    mix = lambda bi, j: (bi, cur(j), 0)
    lag = lambda bi, j: (bi, jnp.maximum(j - 1, 0), 0)
    prv = lambda bi, j: (bi, jnp.maximum(cur(j) * MIX_CHUNKS - 1, 0), 0)
    nxt = lambda bi, j: (bi, jnp.minimum((cur(j) + 1) * MIX_CHUNKS, nblk - 1), 0)
    smem = pl.BlockSpec(memory_space=pltpu.SMEM)
    kv_main = pl.BlockSpec((None, MIX_TILE, D_KV), mix)
    kv_prev = pl.BlockSpec((None, CHUNK, D_KV), prv)
    kv_next = pl.BlockSpec((None, CHUNK, D_KV), nxt)
    resident = (2 * (3 * D_MODEL * D_FF + (D_SSM + D_ATTN) * D_MODEL)
                + 2 * 2 * MIX_TILE * (CONV_DIM + 2 * D_SSM + D_ATTN + 2 * D_KV) + 2 * 4 * MIX_TILE * DT_PAD
                + 2 * 2 * 4 * MIX_TILE * D_MODEL + 2 * MIX_TILE * (D_SSM + D_ATTN) + 4 * D_STATE * D_SSM)
    return pl.pallas_call(
        _mix_out_kernel,
        grid=(b, n_tiles + 1),
        in_specs=[smem, smem,
                  pl.BlockSpec((None, MIX_TILE, CONV_DIM), mix),
                  pl.BlockSpec((None, MIX_TILE, DT_PAD), mix),
                  pl.BlockSpec((None, MIX_TILE, D_SSM), mix),
                  pl.BlockSpec((None, MIX_CHUNKS, D_STATE, D_SSM), lambda bi, j: (bi, cur(j), 0, 0)),
                  pl.BlockSpec((None, MIX_TILE, D_ATTN), mix),
                  kv_main, kv_prev, kv_next, kv_main, kv_prev, kv_next,
                  pl.BlockSpec((None, MIX_TILE, D_MODEL), lag),
                  _resident((1, DT_PAD)), _resident((1, DT_PAD)),
                  _resident((1, D_SSM)), _resident((1, D_SSM)),
                  _resident((D_SSM, D_MODEL)), _resident((D_ATTN, D_MODEL)), _resident((1, D_MODEL)),
                  _resident((D_MODEL, D_FF)), _resident((D_MODEL, D_FF)), _resident((D_FF, D_MODEL)),
                  _resident((1, D_MODEL))],
        out_specs=pl.BlockSpec((None, MIX_TILE, D_MODEL), lag),
        out_shape=jax.ShapeDtypeStruct((b, s, D_MODEL), F32),
        scratch_shapes=[pltpu.VMEM((MIX_TILE, D_SSM), BF16), pltpu.VMEM((MIX_TILE, D_ATTN), BF16),
                        pltpu.VMEM((D_STATE, D_SSM), F32),
                        pltpu.VMEM((MIX_TILE, D_SSM), BF16), pltpu.VMEM((MIX_TILE, D_ATTN), BF16),
                        pltpu.VMEM((MIX_TILE, D_MODEL), F32), pltpu.VMEM((MIX_TILE, D_MODEL), BF16),
                        pltpu.VMEM((MIX_TILE, D_MODEL), F32)],
        compiler_params=pltpu.CompilerParams(
            dimension_semantics=("arbitrary", "arbitrary"), vmem_limit_bytes=_vmem_limit(resident)),
        name="mix_out",
    )(slopes2, sink, u, dt, z, prevb, q, k, k, k, v, v, v, x1, bias, a_log, dskip, nw,
      wos, woa, nw2, wg, wu, wd, nwf)


def _head_selection(offset):
    e = np.zeros((2 * LANES, D_SSM), np.float32)
    for h in range(SSM_HEADS):
        e[offset + h, h * SSM_HEAD_DIM:(h + 1) * SSM_HEAD_DIM] = 1.0
        e[LANES + offset + h, h * SSM_HEAD_DIM:(h + 1) * SSM_HEAD_DIM] = 1.0
    return jnp.asarray(e, BF16)


def _prepare(norm_ffn1_w, ffn1_w_gate, ffn1_w_up, ffn1_w_down, norm_mix_w, w_in, conv_w, conv_b,
             dt_bias_fwd, dt_bias_bwd, a_log_fwd, a_log_bwd, d_skip, ssm_norm_w, attn_sink, w_out,
             norm_ffn2_w, ffn2_w_gate, ffn2_w_up, ffn2_w_down, norm_final_w):
    row = lambda v: v.reshape(1, -1).astype(F32)
    splits = np.cumsum((D_SSM, CONV_DIM, SSM_HEADS, SSM_HEADS, D_ATTN, D_KV))
    wz, wx, wdtf, wdtb, wq, wk, wv = jnp.split(w_in, splits, axis=1)
    wdt = jnp.concatenate([wdtf, wdtb, jnp.zeros((D_MODEL, DT_PAD - 2 * SSM_HEADS), F32)], axis=1)
    order = np.asarray(ATTN_HEAD_ORDER)
    cols = (order[:, None] * HEAD_DIM + np.arange(HEAD_DIM)[None, :]).reshape(-1)
    wq = wq[:, cols]
    pad_dt = jnp.zeros((DT_PAD - 2 * SSM_HEADS,), F32)
    slopes = jnp.exp2(-(8.0 / N_HEADS) * jnp.arange(1, N_HEADS + 1, dtype=F32))
    return dict(
        nw1=row(norm_ffn1_w), wg1=ffn1_w_gate.astype(BF16), wu1=ffn1_w_up.astype(BF16),
        wd1=ffn1_w_down.astype(BF16),
        nwm=row(norm_mix_w), wz=wz.astype(BF16), wx=wx.astype(BF16), wdt=wdt.astype(BF16),
        wq=wq.astype(BF16), wk=wk.astype(BF16), wv=wv.astype(BF16),
        cw=jnp.concatenate([conv_w, jnp.zeros((SUBLANES - CONV_K, CONV_DIM), F32)], axis=0),
        cb=row(conv_b),
        dt_bias=row(jnp.concatenate([dt_bias_fwd, dt_bias_bwd, pad_dt])),
        a_log=row(jnp.concatenate([a_log_fwd, a_log_bwd, pad_dt])),
        e2b=_head_selection(SSM_HEADS),
        dskip=row(jnp.repeat(d_skip, SSM_HEAD_DIM)), nws=row(ssm_norm_w),
        slopes2=(slopes * LOG2E)[order], sink=attn_sink.astype(F32)[order],
        wos=w_out[:D_SSM].astype(BF16), woa=w_out[D_SSM:][cols].astype(BF16),
        nw2=row(norm_ffn2_w), wg2=ffn2_w_gate.astype(BF16), wu2=ffn2_w_up.astype(BF16),
        wd2=ffn2_w_down.astype(BF16), nwf=row(norm_final_w),
    )


def _trunk(x, p):
    b, s, _ = x.shape
    assert s % SEQ_TILE == 0 and (b * s) % TOK_TILE == 0 and x.shape[2] == D_MODEL
    n = b * s
    x1 = _ffn1(x.reshape(n, D_MODEL), p["nw1"], p["wg1"], p["wu1"], p["wd1"])
    z, u, dt, q, k, v, prevb = _mix_in(
        x1.reshape(b, s, D_MODEL), p["nwm"], p["wz"], p["wx"], p["wdt"], p["wq"], p["wk"], p["wv"],
        p["cw"], p["cb"], p["dt_bias"], p["a_log"], p["e2b"])
    return _mix_out(u, dt, z, prevb, q, k, v, x1.reshape(b, s, D_MODEL), p["slopes2"], p["sink"],
                    p["dt_bias"], p["a_log"], p["dskip"], p["nws"],
                    p["wos"], p["woa"], p["nw2"], p["wg2"], p["wu2"], p["wd2"], p["nwf"])


def kernel(x_prompt, x_sample, norm_ffn1_w, ffn1_w_gate, ffn1_w_up, ffn1_w_down, norm_mix_w, w_in, conv_w, conv_b, dt_bias_fwd, dt_bias_bwd, a_log_fwd, a_log_bwd, d_skip, ssm_norm_w, attn_sink, w_out, norm_ffn2_w, ffn2_w_gate, ffn2_w_up, ffn2_w_down, norm_final_w):
    p = _prepare(norm_ffn1_w[0], ffn1_w_gate[0], ffn1_w_up[0], ffn1_w_down[0], norm_mix_w[0], w_in[0],
                 conv_w[0], conv_b[0], dt_bias_fwd[0], dt_bias_bwd[0], a_log_fwd[0], a_log_bwd[0],
                 d_skip[0], ssm_norm_w[0], attn_sink[0], w_out[0], norm_ffn2_w[0], ffn2_w_gate[0],
                 ffn2_w_up[0], ffn2_w_down[0], norm_final_w)
    return (_trunk(x_prompt, p), _trunk(x_sample, p))
```

```python
import math

import jax
import jax.numpy as jnp
import numpy as np
from jax import lax
from jax.experimental import pallas as pl
from jax.experimental.pallas import tpu as pltpu

F32 = jnp.float32
BF16 = jnp.bfloat16

D_MODEL = 1024
D_FF = 2816
SSM_HEADS = 16
SSM_HEAD_DIM = 64
D_SSM = SSM_HEADS * SSM_HEAD_DIM
SSM_GROUPS = 2
D_STATE = 128
CONV_K = 5
CONV_DIM = D_SSM + 2 * SSM_GROUPS * D_STATE
N_HEADS = 16
KV_HEADS = 4
HEAD_DIM = 64
D_ATTN = N_HEADS * HEAD_DIM
D_KV = KV_HEADS * HEAD_DIM
WINDOW = 128
EPS = 1e-6
LOG2E = math.log2(math.e)
Q_SCALE = LOG2E / math.sqrt(HEAD_DIM)

LANES = 128
SUBLANES = 8
VMEM_BYTES_V7X = 64 * 1024 * 1024

CHUNK = LANES
TOK_TILE = 512
SEQ_TILE = 512
CHUNKS_PER_STEP = SEQ_TILE // CHUNK
MIX_TILE = 256
MIX_CHUNKS = MIX_TILE // CHUNK
MXU_TILE = 256
FF_SPLIT = (D_FF // MXU_TILE + 1) // 2 * MXU_TILE
FF_CHUNK = FF_SPLIT
HALO = 2 * SUBLANES
CONV_PIECE = MXU_TILE
DT_PAD = LANES
HEADS_PER_GROUP = SSM_HEADS // SSM_GROUPS
GROUP_W = HEADS_PER_GROUP * SSM_HEAD_DIM

ATTN_HEAD_ORDER = (0, 4, 1, 5, 2, 6, 3, 7, 8, 12, 9, 13, 10, 14, 11, 15)


def _vmem_limit(resident_bytes):
    return int(min(VMEM_BYTES_V7X - 8 * 1024 * 1024, 2 * resident_bytes + 8 * 1024 * 1024))


def _resident(shape):
    nd = len(shape)
    return pl.BlockSpec(shape, lambda *_: (0,) * nd, pipeline_mode=pl.Buffered(1))


def _dot(a, b):
    return jnp.dot(a, b, preferred_element_type=F32)


def _dot_nt(a, b):
    return lax.dot_general(a, b, (((1,), (1,)), ((), ())), preferred_element_type=F32)


def _dot_tn(a, b):
    return lax.dot_general(a, b, (((0,), (0,)), ((), ())), preferred_element_type=F32)


def _rmsnorm(x, w):
    return x * lax.rsqrt(jnp.mean(x * x, axis=-1, keepdims=True) + EPS) * w


def _silu(x):
    return x * jax.nn.sigmoid(x)


def _softplus(x):
    return jnp.maximum(x, 0.0) + jnp.log1p(jnp.exp(-jnp.abs(x)))


def _split2(x):
    hi = x.astype(BF16)
    lo = (x - hi.astype(F32)).astype(BF16)
    return hi, lo


def _split3(x):
    hi = x.astype(BF16)
    r = x - hi.astype(F32)
    mid = r.astype(BF16)
    lo = (r - mid.astype(F32)).astype(BF16)
    return hi, mid, lo


def _swiglu(xn, wg_ref, wu_ref, wd_ref):
    acc = None
    for cols in (slice(0, FF_SPLIT), slice(FF_SPLIT, D_FF)):
        g = _dot(xn, wg_ref[:, cols])
        u = _dot(xn, wu_ref[:, cols])
        part = _dot((_silu(g) * u).astype(BF16), wd_ref[cols, :])
        acc = part if acc is None else acc + part
    return acc


def _ffn1_kernel(x_ref, nw_ref, wg_ref, wu_ref, wd_ref, o_ref):
    x = x_ref[...]
    xn = _rmsnorm(x, nw_ref[...]).astype(BF16)
    o_ref[...] = x + 0.5 * _swiglu(xn, wg_ref, wu_ref, wd_ref)


def _ffn1(x2d, nw, wg, wu, wd):
    n = x2d.shape[0]
    tile = pl.BlockSpec((TOK_TILE, D_MODEL), lambda i: (i, 0))
    resident = 2 * 3 * D_MODEL * D_FF + 4 * 4 * TOK_TILE * D_MODEL + 3 * 4 * TOK_TILE * FF_CHUNK
    return pl.pallas_call(
        _ffn1_kernel,
        grid=(n // TOK_TILE,),
        in_specs=[tile, _resident((1, D_MODEL)), _resident((D_MODEL, D_FF)),
                  _resident((D_MODEL, D_FF)), _resident((D_FF, D_MODEL))],
        out_specs=tile,
        out_shape=jax.ShapeDtypeStruct((n, D_MODEL), F32),
        compiler_params=pltpu.CompilerParams(
            dimension_semantics=("parallel",), vmem_limit_bytes=_vmem_limit(resident)),
        name="ffn1",
    )(x2d, nw, wg, wu, wd)


def _tri(lower):
    r = lax.broadcasted_iota(jnp.int32, (CHUNK, CHUNK), 0)
    c = lax.broadcasted_iota(jnp.int32, (CHUNK, CHUNK), 1)
    return jnp.where((r >= c) if lower else (r <= c), 1.0, 0.0).astype(BF16)


def _tri_matmul(tri, x):
    hi, mid, lo = _split3(x)
    out = _dot(tri, jnp.concatenate([hi, mid, lo], axis=1))
    w = x.shape[1]
    return out[:, 0:w] + out[:, w:2 * w] + out[:, 2 * w:3 * w]


def _neg_exp_row(a_log_ref):
    lane = lax.broadcasted_iota(jnp.int32, (1, DT_PAD), 1)
    return jnp.where(lane < 2 * SSM_HEADS, -jnp.exp(a_log_ref[...]), 0.0)


def _expand_heads(x, e2_ref):
    hi, lo = _split2(x)
    return _dot(jnp.concatenate([hi, lo], axis=1), e2_ref[...])


def _mix_in_kernel(xm_ref, xp_ref, xn_ref, nw_ref, wz_ref, wx_ref, wdt_ref, wq_ref, wk_ref, wv_ref,
                   cw_ref, cb_ref, bias_ref, a_ref, e2b_ref,
                   z_ref, u_ref, dt_ref, q_ref, k_ref, v_ref, prev_ref, pad_ref, st_ref):
    j = pl.program_id(1)
    last = pl.num_programs(1) - 1

    @pl.when(j == 0)
    def _():
        st_ref[...] = jnp.zeros_like(st_ref)

    h = _rmsnorm(xm_ref[...], nw_ref[...]).astype(BF16)
    h_wide = jnp.concatenate([_rmsnorm(xp_ref[...], nw_ref[...]).astype(BF16), h,
                              _rmsnorm(xn_ref[...], nw_ref[...]).astype(BF16)], axis=0)
    dt = _dot(h, wdt_ref[...])
    dt_ref[...] = dt

    def project(w_ref, o_ref, cols, scale=None):
        def run():
            y = _dot(h, w_ref[:, cols])
            o_ref[:, cols] = (y if scale is None else y * scale).astype(BF16)
        return run

    tiles = lambda width: [slice(t * MXU_TILE, (t + 1) * MXU_TILE) for t in range(width // MXU_TILE)]
    fillers = ([project(wz_ref, z_ref, cols) for cols in tiles(D_SSM)]
               + [project(wq_ref, q_ref, cols, Q_SCALE) for cols in tiles(D_ATTN)]
               + [project(wk_ref, k_ref, cols) for cols in tiles(D_KV)]
               + [project(wv_ref, v_ref, cols) for cols in tiles(D_KV)])

    def fill(count=1):
        for _ in range(count):
            if fillers:
                fillers.pop(0)()

    first_tap = HALO - (CONV_K - 1) // 2
    n_pieces = CONV_DIM // CONV_PIECE

    def project_conv_input(piece):
        cols = slice(piece * CONV_PIECE, (piece + 1) * CONV_PIECE)
        xbc = _dot(h_wide, wx_ref[:, cols])
        pad_ref[0:HALO, cols] = jnp.where(j < last, xbc[0:HALO, :], 0.0)
        pad_ref[HALO:HALO + SEQ_TILE, cols] = xbc[HALO:HALO + SEQ_TILE, :]
        pad_ref[HALO + SEQ_TILE:2 * HALO + SEQ_TILE, cols] = jnp.where(
            j > 0, xbc[HALO + SEQ_TILE:2 * HALO + SEQ_TILE, :], 0.0)

    def conv(piece):
        cols = slice(piece * CONV_PIECE, (piece + 1) * CONV_PIECE)
        for c in range(CHUNKS_PER_STEP):
            acc = cb_ref[:, cols]
            for k in range(CONV_K):
                start = c * CHUNK + first_tap + k
                acc = acc + pad_ref[start:start + CHUNK, cols] * cw_ref[k:k + 1, cols]
            u_ref[c * CHUNK:(c + 1) * CHUNK, cols] = _silu(acc).astype(BF16)

    project_conv_input(0)
    for piece in range(n_pieces):
        if piece + 1 < n_pieces:
            project_conv_input(piece + 1)
        conv(piece)
        fill()

    tri_u = _tri(lower=False)
    a_row = _neg_exp_row(a_ref)
    chunk_rows = [slice(c * CHUNK, (c + 1) * CHUNK) for c in range(CHUNKS_PER_STEP)]
    dtvs = [_softplus(dt[r, :] + bias_ref[...]) for r in chunk_rows]
    rcss = [_tri_matmul(tri_u, dtv * a_row) for dtv in dtvs]
    fill()
    w_exps = [_expand_heads(jnp.exp(rcs[0:1, :] - rcs) * dtv, e2b_ref) for rcs, dtv in zip(rcss, dtvs)]
    decs = [_expand_heads(jnp.broadcast_to(jnp.exp(rcs[0:1, :]), (SUBLANES, LANES)), e2b_ref)[0:1, :]
            for rcs in rcss]
    fill()
    contribs = []
    for r, w_exp in zip(chunk_rows, w_exps):
        xsw = (u_ref[r, 0:D_SSM].astype(F32) * w_exp).astype(BF16)
        contribs.append([_dot_tn(u_ref[r, D_SSM + g * D_STATE:D_SSM + (g + 1) * D_STATE],
                                 xsw[:, g * GROUP_W:(g + 1) * GROUP_W]) for g in range(SSM_GROUPS)])
        fill()
    fill(len(fillers))
    for c in reversed(range(CHUNKS_PER_STEP)):
        prev_ref[c] = st_ref[...].astype(BF16)
        for g in range(SSM_GROUPS):
            cols = slice(g * GROUP_W, (g + 1) * GROUP_W)
            st_ref[:, cols] = st_ref[:, cols] * decs[c][:, cols] + contribs[c][g]


def _mix_in(x1, nw, wz, wx, wdt, wq, wk, wv, cw, cb, bias, a_log, e2b):
    b, s, _ = x1.shape
    steps = s // SEQ_TILE
    halo_per_tile = SEQ_TILE // HALO
    n_halo = s // HALO
    widths = (D_SSM, CONV_DIM, DT_PAD, D_ATTN, D_KV, D_KV)
    dtypes = (BF16, BF16, F32, BF16, BF16, BF16)
    rev = lambda bi, j: (bi, steps - 1 - j, 0)
    prv = lambda bi, j: (bi, jnp.maximum((steps - 1 - j) * halo_per_tile - 1, 0), 0)
    nxt = lambda bi, j: (bi, jnp.minimum((steps - j) * halo_per_tile, n_halo - 1), 0)
    resident = (2 * D_MODEL * sum(widths) + 2 * 2 * LANES * D_SSM
                + 2 * 4 * SEQ_TILE * (D_MODEL + sum(widths)) + 2 * 2 * SEQ_TILE * D_SSM
                + 4 * (SEQ_TILE + 2 * HALO) * CONV_DIM + 4 * D_STATE * D_SSM)
    return pl.pallas_call(
        _mix_in_kernel,
        grid=(b, steps),
        in_specs=[pl.BlockSpec((None, SEQ_TILE, D_MODEL), rev),
                  pl.BlockSpec((None, HALO, D_MODEL), prv),
                  pl.BlockSpec((None, HALO, D_MODEL), nxt),
                  _resident((1, D_MODEL))] + [_resident((D_MODEL, w)) for w in widths]
                 + [_resident((SUBLANES, CONV_DIM)), _resident((1, CONV_DIM)),
                    _resident((1, DT_PAD)), _resident((1, DT_PAD)), _resident((2 * LANES, D_SSM))],
        out_specs=[pl.BlockSpec((None, SEQ_TILE, w), rev) for w in widths]
                  + [pl.BlockSpec((None, CHUNKS_PER_STEP, D_STATE, D_SSM),
                                  lambda bi, j: (bi, steps - 1 - j, 0, 0))],
        out_shape=[jax.ShapeDtypeStruct((b, s, w), d) for w, d in zip(widths, dtypes)]
                  + [jax.ShapeDtypeStruct((b, s // CHUNK, D_STATE, D_SSM), BF16)],
        scratch_shapes=[pltpu.VMEM((SEQ_TILE + 2 * HALO, CONV_DIM), F32),
                        pltpu.VMEM((D_STATE, D_SSM), F32)],
        compiler_params=pltpu.CompilerParams(
            dimension_semantics=("arbitrary", "arbitrary"), vmem_limit_bytes=_vmem_limit(resident)),
        name="mix_in",
    )(x1, x1, x1, nw, wz, wx, wdt, wq, wk, wv, cw, cb, bias, a_log, e2b)


def _ssd_prologue(c, u_ref, dt_ref, bias_ref, a_row, tri_both, col):
    rows = slice(c * CHUNK, (c + 1) * CHUNK)
    dtv = _softplus(dt_ref[rows, :] + bias_ref[...])
    both = _tri_matmul(tri_both, dtv * a_row)
    cs = both[0:CHUNK, :]
    rcs = both[CHUNK:2 * CHUNK, :]
    comb = jnp.where(col < SSM_HEADS, cs, rcs)
    comb_t = comb.T
    dt_t = dtv.T

    xs = u_ref[rows, 0:D_SSM].astype(F32)
    bmat = [u_ref[rows, D_SSM + g * D_STATE:D_SSM + (g + 1) * D_STATE] for g in range(SSM_GROUPS)]
    cmat = [u_ref[rows, D_SSM + (SSM_GROUPS + g) * D_STATE:D_SSM + (SSM_GROUPS + g + 1) * D_STATE]
            for g in range(SSM_GROUPS)]
    cb = [_dot_nt(cmat[g], bmat[g]) for g in range(SSM_GROUPS)]
    return dict(dtv=dtv, comb=comb, comb_t=comb_t, dt_t=dt_t, xs=xs, bmat=bmat, cmat=cmat, cb=cb)


def _ssd_head_pair(pro, j, row, col):
    half = SSM_HEAD_DIM
    comb, comb_t, dt_t, dtv = pro["comb"], pro["comb_t"], pro["dt_t"], pro["dtv"]
    g = (2 * j) // HEADS_PER_GROUP
    lane_bcast = lambda a, i: jnp.broadcast_to(a[:, i:i + 1], (CHUNK, CHUNK))
    ms, wf, ef, eb, decf = [], [], [], [], []
    for h in (2 * j, 2 * j + 1):
        csf = lane_bcast(comb, h)
        rcsb = lane_bcast(comb, SSM_HEADS + h)
        df = csf - comb_t[h:h + 1, :]
        db = rcsb - comb_t[SSM_HEADS + h:SSM_HEADS + h + 1, :]
        dtf = dt_t[h:h + 1, :]
        dtb = dt_t[SSM_HEADS + h:SSM_HEADS + h + 1, :]
        dsel = jnp.where(row > col, dtf, jnp.where(row < col, dtb, dtf + dtb))
        decay = jnp.exp(jnp.where(row >= col, df, db))
        ms.append((pro["cb"][g] * decay * dsel).astype(BF16))
        last = csf[CHUNK - 1:CHUNK, :]
        wf.append(jnp.exp(last - csf) * lane_bcast(dtv, h))
        ef.append(jnp.exp(csf))
        eb.append(jnp.exp(rcsb))
        decf.append(jnp.exp(last))
    pair = lambda ab: jnp.where(col[0:ab[0].shape[0], :] < half, ab[0], ab[1])
    xt = pro["xs"][:, j * LANES:(j + 1) * LANES]
    rhs = jnp.concatenate([jnp.where(col < half, xt, 0.0), jnp.where(col >= half, xt, 0.0)],
                          axis=0).astype(BF16)
    ydiag = _dot(jnp.concatenate(ms, axis=1), rhs)
    return dict(ydiag=ydiag, wf=pair(wf), ef=pair(ef), eb=pair(eb), decf=pair(decf))


def _ssd_epilogue(c, pro, pairs, z_ref, prev_ref, dskip_ref, nw_ref, y_ref, st_ref):
    rows = slice(c * CHUNK, (c + 1) * CHUNK)
    xs, bmat, cmat = (pro[n] for n in ("xs", "bmat", "cmat"))
    ydiag, wf, ef, eb, decf = (jnp.concatenate([p[n] for p in pairs], axis=1)
                               for n in ("ydiag", "wf", "ef", "eb", "decf"))
    xsw = (xs * wf).astype(BF16)

    for g in range(SSM_GROUPS):
        cols = slice(g * GROUP_W, (g + 1) * GROUP_W)
        state = st_ref[:, cols]
        yoff_f = _dot(cmat[g], state.astype(BF16))
        yoff_b = _dot(cmat[g], prev_ref[c, :, cols])
        y = (ydiag[:, cols] + ef[:, cols] * yoff_f + eb[:, cols] * yoff_b
             + xs[:, cols] * dskip_ref[:, cols])
        y = y * _silu(z_ref[rows, cols].astype(F32))
        y = y * lax.rsqrt(jnp.mean(y * y, axis=-1, keepdims=True) + EPS) * nw_ref[:, cols]
        y_ref[rows, cols] = y.astype(BF16)
        st_ref[:, cols] = state * decf[:, cols] + _dot_tn(bmat[g], xsw[:, cols])


ATTN_KEYS = 3 * CHUNK
Q_TILES_PER_KV_TILE = (D_ATTN // LANES) // (D_KV // LANES)


def _attn_prologue(c, n_chunks, blk, last_blk, k_refs, v_refs):
    keys = ATTN_KEYS
    half = HEAD_DIM
    rows = slice(c * CHUNK, (c + 1) * CHUNK)

    srow = lax.broadcasted_iota(jnp.int32, (keys, CHUNK), 0)
    tcol = lax.broadcasted_iota(jnp.int32, (keys, CHUNK), 1)
    idist = jnp.abs(tcol + CHUNK - srow)
    lo_key = jnp.where(blk == 0, CHUNK, 0)
    hi_key = jnp.where(blk == last_blk, 2 * CHUNK, keys)
    valid = (idist <= WINDOW) & (srow >= lo_key) & (srow < hi_key)
    neg_dist = jnp.where(valid, -idist.astype(F32), -jnp.inf)
    klane = lax.broadcasted_iota(jnp.int32, (keys, LANES), 1)
    vrow = lax.broadcasted_iota(jnp.int32, (LANES, keys), 0)
    srow16 = lax.broadcasted_iota(jnp.int32, (2 * SUBLANES, 2 * keys), 0)
    scol16 = lax.broadcasted_iota(jnp.int32, (2 * SUBLANES, 2 * keys), 1)
    sum_rows = jnp.where((srow16 < SUBLANES) == (scol16 < keys), 1.0, 0.0).astype(BF16)

    def window(refs, lanes):
        main_ref, prev_ref, next_ref = refs
        prev = prev_ref[:, lanes] if c == 0 else main_ref[(c - 1) * CHUNK:c * CHUNK, lanes]
        nxt = (next_ref[:, lanes] if c == n_chunks - 1
               else main_ref[(c + 1) * CHUNK:(c + 2) * CHUNK, lanes])
        return jnp.concatenate([prev, main_ref[rows, lanes], nxt], axis=0)

    kk, vv = [], []
    for m in range(D_KV // LANES):
        lanes = slice(m * LANES, (m + 1) * LANES)
        kf = window(k_refs, lanes).astype(F32)
        kk.append(jnp.concatenate([jnp.where(klane < half, kf, 0.0), jnp.where(klane >= half, kf, 0.0)],
                                  axis=0).astype(BF16))
        vt_t = window(v_refs, lanes).astype(F32).T
        vm = jnp.concatenate([jnp.where(vrow < half, vt_t, 0.0), jnp.where(vrow >= half, vt_t, 0.0)],
                             axis=1).astype(BF16)
        vv.append(jnp.concatenate([vm, sum_rows], axis=0))
    return dict(neg_dist=neg_dist, kk=kk, vv=vv)


Q_TILES_PER_CALL = MXU_TILE // LANES
assert Q_TILES_PER_KV_TILE % Q_TILES_PER_CALL == 0


def _attn_scores(c, apro, i, q_ref):
    rows = slice(c * CHUNK, (c + 1) * CHUNK)
    kv_tile = i * Q_TILES_PER_CALL // Q_TILES_PER_KV_TILE
    q_stack = jnp.concatenate([q_ref[rows, j * LANES:(j + 1) * LANES]
                               for j in range(Q_TILES_PER_CALL * i, Q_TILES_PER_CALL * (i + 1))], axis=0)
    return _dot_nt(apro["kk"][kv_tile], q_stack)


def _attn_finish(c, apro, i, st, slope_ref, sink_ref, o_ref):
    keys = ATTN_KEYS
    half = HEAD_DIM
    rows = slice(c * CHUNK, (c + 1) * CHUNK)
    kv_tile = i * Q_TILES_PER_CALL // Q_TILES_PER_KV_TILE
    orow = lax.broadcasted_iota(jnp.int32, (LANES, CHUNK), 0)
    ps, mxs = [], []
    for t in range(Q_TILES_PER_CALL):
        p_tile = []
        for e in range(2):
            slot = 2 * (Q_TILES_PER_CALL * i + t) + e
            s = st[e * keys:(e + 1) * keys, t * CHUNK:(t + 1) * CHUNK] + slope_ref[slot] * apro["neg_dist"]
            mx = jnp.maximum(jnp.max(s, axis=0, keepdims=True), sink_ref[slot] * LOG2E)
            p_tile.append(jnp.exp2(s - mx).astype(BF16))
            mxs.append(mx)
        ps.append(jnp.concatenate(p_tile, axis=0))
    ot = _dot(apro["vv"][kv_tile], jnp.concatenate(ps, axis=1))
    for t in range(Q_TILES_PER_CALL):
        j = Q_TILES_PER_CALL * i + t
        ot_t = ot[:, t * CHUNK:(t + 1) * CHUNK]
        inv = [1.0 / (ot_t[LANES + e * SUBLANES:LANES + e * SUBLANES + 1, :]
                      + jnp.exp2(sink_ref[2 * j + e] * LOG2E - mxs[2 * t + e])) for e in range(2)]
        out = ot_t[0:LANES, :] * jnp.where(orow < half, inv[0], inv[1])
        o_ref[rows, j * LANES:(j + 1) * LANES] = out.T.astype(BF16)


def _out_ffn2_slices(ys, ya, x_ref, wos_ref, woa_ref, nw2_ref, wg_ref, wu_ref, wd_ref, nwf_ref, o_ref,
                     x2_ref, xn_ref, acc_ref):
    v = {}
    tile = lambda t: slice(t * MXU_TILE, (t + 1) * MXU_TILE)

    def out_proj(y_ref, w_ref, t, first):
        def run():
            base = x_ref[...] if first else x2_ref[...]
            x2_ref[...] = base + _dot(y_ref[:, tile(t)], w_ref[tile(t), :])
        return run

    def gate_up(t):
        def run():
            if t == 0:
                xn_ref[...] = _rmsnorm(x2_ref[...], nw2_ref[...]).astype(BF16)
            v["g", t] = _dot(xn_ref[...], wg_ref[:, tile(t)])
            v["u", t] = _dot(xn_ref[...], wu_ref[:, tile(t)])
        return run

    def down(t):
        def run():
            hid = (_silu(v.pop(("g", t))) * v.pop(("u", t))).astype(BF16)
            part = _dot(hid, wd_ref[tile(t), :])
            acc_ref[...] = part if t == 0 else acc_ref[...] + part
        return run

    def finish():
        o_ref[...] = _rmsnorm(x2_ref[...] + 0.5 * acc_ref[...], nwf_ref[...])

    n_y = D_SSM // MXU_TILE
    n_ff = D_FF // MXU_TILE
    items = ([out_proj(ys, wos_ref, t, t == 0) for t in range(n_y)]
             + [out_proj(ya, woa_ref, t, False) for t in range(D_ATTN // MXU_TILE)])
    for t in range(n_ff):
        items.append(gate_up(t))
        if t > 0:
            items.append(down(t - 1))
    items += [down(n_ff - 1), finish]
    return items


def _mix_out_kernel(slope_ref, sink_ref, u_ref, dt_ref, z_ref, prev_ref, q_ref,
                    km_ref, kp_ref, kn_ref, vm_ref, vp_ref, vn_ref, x_ref,
                    bias_ref, a_ref, dskip_ref, nw_ref,
                    wos_ref, woa_ref, nw2_ref, wg_ref, wu_ref, wd_ref, nwf_ref,
                    o_ref, ys_ref, ya_ref, st_ref, yps_ref, ypa_ref, x2_ref, xn_ref, acc_ref):
    j = pl.program_id(1)
    n_tiles = pl.num_programs(1) - 1

    @pl.when(j == 0)
    def _():
        st_ref[...] = jnp.zeros_like(st_ref)

    def output_stage_slices():
        yps_ref[...] = ys_ref[...]
        ypa_ref[...] = ya_ref[...]
        return _out_ffn2_slices(yps_ref, ypa_ref, x_ref, wos_ref, woa_ref, nw2_ref,
                                wg_ref, wu_ref, wd_ref, nwf_ref, o_ref, x2_ref, xn_ref, acc_ref)

    def mixers(fillers):
        def fill(count=1):
            for _ in range(count):
                if fillers:
                    fillers.pop(0)()

        tri_both = jnp.concatenate([_tri(lower=True), _tri(lower=False)], axis=0)
        row = lax.broadcasted_iota(jnp.int32, (CHUNK, CHUNK), 0)
        col = lax.broadcasted_iota(jnp.int32, (CHUNK, CHUNK), 1)
        a_row = _neg_exp_row(a_ref)
        n_pairs = SSM_HEADS // 2
        assert n_pairs == D_ATTN // LANES
        fill(2)
        for c in range(MIX_CHUNKS):
            pro = _ssd_prologue(c, u_ref, dt_ref, bias_ref, a_row, tri_both, col)
            fill()
            apro = _attn_prologue(c, MIX_CHUNKS, j * MIX_CHUNKS + c, n_tiles * MIX_CHUNKS - 1,
                                  (km_ref, kp_ref, kn_ref), (vm_ref, vp_ref, vn_ref))
            fill()
            ydiag = []
            n_calls = n_pairs // Q_TILES_PER_CALL
            st_next = _attn_scores(c, apro, 0, q_ref)
            for i in range(n_calls):
                st = st_next
                if i + 1 < n_calls:
                    st_next = _attn_scores(c, apro, i + 1, q_ref)
                fill()
                for t in range(Q_TILES_PER_CALL):
                    ydiag.append(_ssd_head_pair(pro, Q_TILES_PER_CALL * i + t, row, col))
                    fill()
                _attn_finish(c, apro, i, st, slope_ref, sink_ref, ya_ref)
            fill(1 if c + 1 < MIX_CHUNKS else len(fillers))
            _ssd_epilogue(c, pro, ydiag, z_ref, prev_ref, dskip_ref, nw_ref, ys_ref, st_ref)
        assert not fillers

    @pl.when(j == 0)
    def _():
        mixers([])

    @pl.when(jnp.logical_and(j > 0, j < n_tiles))
    def _():
        mixers(output_stage_slices())

    @pl.when(j == n_tiles)
    def _():
        for run in output_stage_slices():
            run()


def _mix_out(u, dt, z, prevb, q, k, v, x1, slopes2, sink, bias, a_log, dskip, nw,
             wos, woa, nw2, wg, wu, wd, nwf):
    b, s, _ = u.shape
    n_tiles = s // MIX_TILE
    nblk = s // CHUNK
    cur = lambda j: jnp.minimum(j, n_tiles - 1)
    mix = lambda bi, j: (bi, cur(j), 0)
    lag = lambda bi, j: (bi, jnp.maximum(j - 1, 0), 0)
    prv = lambda bi, j: (bi, jnp.maximum(cur(j) * MIX_CHUNKS - 1, 0), 0)
    nxt = lambda bi, j: (bi, jnp.minimum((cur(j) + 1) * MIX_CHUNKS, nblk - 1), 0)
    smem = pl.BlockSpec(memory_space=pltpu.SMEM)
    kv_main = pl.BlockSpec((None, MIX_TILE, D_KV), mix)
    kv_prev = pl.BlockSpec((None, CHUNK, D_KV), prv)
    kv_next = pl.BlockSpec((None, CHUNK, D_KV), nxt)
    resident = (2 * (3 * D_MODEL * D_FF + (D_SSM + D_ATTN) * D_MODEL)
                + 2 * 2 * MIX_TILE * (CONV_DIM + 2 * D_SSM + D_ATTN + 2 * D_KV) + 2 * 4 * MIX_TILE * DT_PAD
                + 2 * 2 * 4 * MIX_TILE * D_MODEL + 2 * MIX_TILE * (D_SSM + D_ATTN) + 4 * D_STATE * D_SSM)
    return pl.pallas_call(
        _mix_out_kernel,
        grid=(b, n_tiles + 1),
        in_specs=[smem, smem,
                  pl.BlockSpec((None, MIX_TILE, CONV_DIM), mix),
                  pl.BlockSpec((None, MIX_TILE, DT_PAD), mix),
                  pl.BlockSpec((None, MIX_TILE, D_SSM), mix),
                  pl.BlockSpec((None, MIX_CHUNKS, D_STATE, D_SSM), lambda bi, j: (bi, cur(j), 0, 0)),
                  pl.BlockSpec((None, MIX_TILE, D_ATTN), mix),
                  kv_main, kv_prev, kv_next, kv_main, kv_prev, kv_next,
                  pl.BlockSpec((None, MIX_TILE, D_MODEL), lag),
                  _resident((1, DT_PAD)), _resident((1, DT_PAD)),
                  _resident((1, D_SSM)), _resident((1, D_SSM)),
                  _resident((D_SSM, D_MODEL)), _resident((D_ATTN, D_MODEL)), _resident((1, D_MODEL)),
                  _resident((D_MODEL, D_FF)), _resident((D_MODEL, D_FF)), _resident((D_FF, D_MODEL)),
                  _resident((1, D_MODEL))],
        out_specs=pl.BlockSpec((None, MIX_TILE, D_MODEL), lag),
        out_shape=jax.ShapeDtypeStruct((b, s, D_MODEL), F32),
        scratch_shapes=[pltpu.VMEM((MIX_TILE, D_SSM), BF16), pltpu.VMEM((MIX_TILE, D_ATTN), BF16),
                        pltpu.VMEM((D_STATE, D_SSM), F32),
                        pltpu.VMEM((MIX_TILE, D_SSM), BF16), pltpu.VMEM((MIX_TILE, D_ATTN), BF16),
                        pltpu.VMEM((MIX_TILE, D_MODEL), F32), pltpu.VMEM((MIX_TILE, D_MODEL), BF16),
                        pltpu.VMEM((MIX_TILE, D_MODEL), F32)],
        compiler_params=pltpu.CompilerParams(
            dimension_semantics=("arbitrary", "arbitrary"), vmem_limit_bytes=_vmem_limit(resident)),
        name="mix_out",
    )(slopes2, sink, u, dt, z, prevb, q, k, k, k, v, v, v, x1, bias, a_log, dskip, nw,
      wos, woa, nw2, wg, wu, wd, nwf)


def _head_selection(offset):
    e = np.zeros((2 * LANES, D_SSM), np.float32)
    for h in range(SSM_HEADS):
        e[offset + h, h * SSM_HEAD_DIM:(h + 1) * SSM_HEAD_DIM] = 1.0
        e[LANES + offset + h, h * SSM_HEAD_DIM:(h + 1) * SSM_HEAD_DIM] = 1.0
    return jnp.asarray(e, BF16)


def _prepare(norm_ffn1_w, ffn1_w_gate, ffn1_w_up, ffn1_w_down, norm_mix_w, w_in, conv_w, conv_b,
             dt_bias_fwd, dt_bias_bwd, a_log_fwd, a_log_bwd, d_skip, ssm_norm_w, attn_sink, w_out,
             norm_ffn2_w, ffn2_w_gate, ffn2_w_up, ffn2_w_down, norm_final_w):
    row = lambda v: v.reshape(1, -1).astype(F32)
    splits = np.cumsum((D_SSM, CONV_DIM, SSM_HEADS, SSM_HEADS, D_ATTN, D_KV))
    wz, wx, wdtf, wdtb, wq, wk, wv = jnp.split(w_in, splits, axis=1)
    wdt = jnp.concatenate([wdtf, wdtb, jnp.zeros((D_MODEL, DT_PAD - 2 * SSM_HEADS), F32)], axis=1)
    order = np.asarray(ATTN_HEAD_ORDER)
    cols = (order[:, None] * HEAD_DIM + np.arange(HEAD_DIM)[None, :]).reshape(-1)
    wq = wq[:, cols]
    pad_dt = jnp.zeros((DT_PAD - 2 * SSM_HEADS,), F32)
    slopes = jnp.exp2(-(8.0 / N_HEADS) * jnp.arange(1, N_HEADS + 1, dtype=F32))
    return dict(
        nw1=row(norm_ffn1_w), wg1=ffn1_w_gate.astype(BF16), wu1=ffn1_w_up.astype(BF16),
        wd1=ffn1_w_down.astype(BF16),
        nwm=row(norm_mix_w), wz=wz.astype(BF16), wx=wx.astype(BF16), wdt=wdt.astype(BF16),
        wq=wq.astype(BF16), wk=wk.astype(BF16), wv=wv.astype(BF16),
        cw=jnp.concatenate([conv_w, jnp.zeros((SUBLANES - CONV_K, CONV_DIM), F32)], axis=0),
        cb=row(conv_b),
        dt_bias=row(jnp.concatenate([dt_bias_fwd, dt_bias_bwd, pad_dt])),
        a_log=row(jnp.concatenate([a_log_fwd, a_log_bwd, pad_dt])),
        e2b=_head_selection(SSM_HEADS),
        dskip=row(jnp.repeat(d_skip, SSM_HEAD_DIM)), nws=row(ssm_norm_w),
        slopes2=(slopes * LOG2E)[order], sink=attn_sink.astype(F32)[order],
        wos=w_out[:D_SSM].astype(BF16), woa=w_out[D_SSM:][cols].astype(BF16),
        nw2=row(norm_ffn2_w), wg2=ffn2_w_gate.astype(BF16), wu2=ffn2_w_up.astype(BF16),
        wd2=ffn2_w_down.astype(BF16), nwf=row(norm_final_w),
    )


def _trunk(x, p):
    b, s, _ = x.shape
    assert s % SEQ_TILE == 0 and (b * s) % TOK_TILE == 0 and x.shape[2] == D_MODEL
    n = b * s
    x1 = _ffn1(x.reshape(n, D_MODEL), p["nw1"], p["wg1"], p["wu1"], p["wd1"])
    z, u, dt, q, k, v, prevb = _mix_in(
        x1.reshape(b, s, D_MODEL), p["nwm"], p["wz"], p["wx"], p["wdt"], p["wq"], p["wk"], p["wv"],
        p["cw"], p["cb"], p["dt_bias"], p["a_log"], p["e2b"])
    return _mix_out(u, dt, z, prevb, q, k, v, x1.reshape(b, s, D_MODEL), p["slopes2"], p["sink"],
                    p["dt_bias"], p["a_log"], p["dskip"], p["nws"],
                    p["wos"], p["woa"], p["nw2"], p["wg2"], p["wu2"], p["wd2"], p["nwf"])


def kernel(x_prompt, x_sample, norm_ffn1_w, ffn1_w_gate, ffn1_w_up, ffn1_w_down, norm_mix_w, w_in, conv_w, conv_b, dt_bias_fwd, dt_bias_bwd, a_log_fwd, a_log_bwd, d_skip, ssm_norm_w, attn_sink, w_out, norm_ffn2_w, ffn2_w_gate, ffn2_w_up, ffn2_w_down, norm_final_w):
    p = _prepare(norm_ffn1_w[0], ffn1_w_gate[0], ffn1_w_up[0], ffn1_w_down[0], norm_mix_w[0], w_in[0],
                 conv_w[0], conv_b[0], dt_bias_fwd[0], dt_bias_bwd[0], a_log_fwd[0], a_log_bwd[0],
                 d_skip[0], ssm_norm_w[0], attn_sink[0], w_out[0], norm_ffn2_w[0], ffn2_w_gate[0],
                 ffn2_w_up[0], ffn2_w_down[0], norm_final_w)
    return (_trunk(x_prompt, p), _trunk(x_sample, p))
```

```python
import math

import jax
import jax.numpy as jnp
import numpy as np
from jax import lax
from jax.experimental import pallas as pl
from jax.experimental.pallas import tpu as pltpu

F32 = jnp.float32
BF16 = jnp.bfloat16

D_MODEL = 1024
D_FF = 2816
SSM_HEADS = 16
SSM_HEAD_DIM = 64
D_SSM = SSM_HEADS * SSM_HEAD_DIM
SSM_GROUPS = 2
D_STATE = 128
CONV_K = 5
CONV_DIM = D_SSM + 2 * SSM_GROUPS * D_STATE
N_HEADS = 16
KV_HEADS = 4
HEAD_DIM = 64
D_ATTN = N_HEADS * HEAD_DIM
D_KV = KV_HEADS * HEAD_DIM
WINDOW = 128
EPS = 1e-6
LOG2E = math.log2(math.e)
Q_SCALE = LOG2E / math.sqrt(HEAD_DIM)

LANES = 128
SUBLANES = 8
VMEM_BYTES_V7X = 64 * 1024 * 1024

CHUNK = LANES
TOK_TILE = 1024
SEQ_TILE = 512
CHUNKS_PER_STEP = SEQ_TILE // CHUNK
MIX_TILE = 256
MIX_CHUNKS = MIX_TILE // CHUNK
MXU_TILE = 256
FF_SPLIT = (D_FF // MXU_TILE + 1) // 2 * MXU_TILE
FF_CHUNK = FF_SPLIT
HALO = 2 * SUBLANES
CONV_PIECE = MXU_TILE
DT_PAD = LANES
HEADS_PER_GROUP = SSM_HEADS // SSM_GROUPS
GROUP_W = HEADS_PER_GROUP * SSM_HEAD_DIM

ATTN_HEAD_ORDER = (0, 4, 1, 5, 2, 6, 3, 7, 8, 12, 9, 13, 10, 14, 11, 15)


def _vmem_limit(resident_bytes):
    return int(min(VMEM_BYTES_V7X - 8 * 1024 * 1024, 2 * resident_bytes + 8 * 1024 * 1024))


def _resident(shape):
    nd = len(shape)
    return pl.BlockSpec(shape, lambda *_: (0,) * nd, pipeline_mode=pl.Buffered(1))


def _dot(a, b):
    return jnp.dot(a, b, preferred_element_type=F32)


def _dot_nt(a, b):
    return lax.dot_general(a, b, (((1,), (1,)), ((), ())), preferred_element_type=F32)


def _dot_tn(a, b):
    return lax.dot_general(a, b, (((0,), (0,)), ((), ())), preferred_element_type=F32)


def _rmsnorm(x, w):
    return x * lax.rsqrt(jnp.mean(x * x, axis=-1, keepdims=True) + EPS) * w


def _silu(x):
    return x * jax.nn.sigmoid(x)


def _softplus(x):
    return jnp.maximum(x, 0.0) + jnp.log1p(jnp.exp(-jnp.abs(x)))


def _split2(x):
    hi = x.astype(BF16)
    lo = (x - hi.astype(F32)).astype(BF16)
    return hi, lo


def _split3(x):
    hi = x.astype(BF16)
    r = x - hi.astype(F32)
    mid = r.astype(BF16)
    lo = (r - mid.astype(F32)).astype(BF16)
    return hi, mid, lo


def _swiglu(xn, wg_ref, wu_ref, wd_ref):
    acc = None
    for cols in (slice(0, FF_SPLIT), slice(FF_SPLIT, D_FF)):
        g = _dot(xn, wg_ref[:, cols])
        u = _dot(xn, wu_ref[:, cols])
        part = _dot((_silu(g) * u).astype(BF16), wd_ref[cols, :])
        acc = part if acc is None else acc + part
    return acc


def _ffn1_kernel(x_ref, nw_ref, wg_ref, wu_ref, wd_ref, o_ref):
    x = x_ref[...]
    xn = _rmsnorm(x, nw_ref[...]).astype(BF16)
    o_ref[...] = x + 0.5 * _swiglu(xn, wg_ref, wu_ref, wd_ref)


def _ffn1(x2d, nw, wg, wu, wd):
    n = x2d.shape[0]
    tile = pl.BlockSpec((TOK_TILE, D_MODEL), lambda i: (i, 0))
    resident = 2 * 3 * D_MODEL * D_FF + 4 * 4 * TOK_TILE * D_MODEL + 3 * 4 * TOK_TILE * FF_CHUNK
    return pl.pallas_call(
        _ffn1_kernel,
        grid=(n // TOK_TILE,),
        in_specs=[tile, _resident((1, D_MODEL)), _resident((D_MODEL, D_FF)),
                  _resident((D_MODEL, D_FF)), _resident((D_FF, D_MODEL))],
        out_specs=tile,
        out_shape=jax.ShapeDtypeStruct((n, D_MODEL), F32),
        compiler_params=pltpu.CompilerParams(
            dimension_semantics=("parallel",), vmem_limit_bytes=_vmem_limit(resident)),
        name="ffn1",
    )(x2d, nw, wg, wu, wd)


def _tri(lower):
    r = lax.broadcasted_iota(jnp.int32, (CHUNK, CHUNK), 0)
    c = lax.broadcasted_iota(jnp.int32, (CHUNK, CHUNK), 1)
    return jnp.where((r >= c) if lower else (r <= c), 1.0, 0.0).astype(BF16)


def _tri_matmul(tri, x):
    hi, mid, lo = _split3(x)
    out = _dot(tri, jnp.concatenate([hi, mid, lo], axis=1))
    w = x.shape[1]
    return out[:, 0:w] + out[:, w:2 * w] + out[:, 2 * w:3 * w]


def _neg_exp_row(a_log_ref):
    lane = lax.broadcasted_iota(jnp.int32, (1, DT_PAD), 1)
    return jnp.where(lane < 2 * SSM_HEADS, -jnp.exp(a_log_ref[...]), 0.0)


def _expand_heads(x, e2_ref):
    hi, lo = _split2(x)
    return _dot(jnp.concatenate([hi, lo], axis=1), e2_ref[...])


def _mix_in_kernel(xm_ref, xp_ref, xn_ref, nw_ref, wz_ref, wx_ref, wdt_ref, wq_ref, wk_ref, wv_ref,
                   cw_ref, cb_ref, bias_ref, a_ref, e2b_ref,
                   z_ref, u_ref, dt_ref, q_ref, k_ref, v_ref, prev_ref, pad_ref, st_ref):
    j = pl.program_id(1)
    last = pl.num_programs(1) - 1

    @pl.when(j == 0)
    def _():
        st_ref[...] = jnp.zeros_like(st_ref)

    h = _rmsnorm(xm_ref[...], nw_ref[...]).astype(BF16)
    h_wide = jnp.concatenate([_rmsnorm(xp_ref[...], nw_ref[...]).astype(BF16), h,
                              _rmsnorm(xn_ref[...], nw_ref[...]).astype(BF16)], axis=0)
    dt = _dot(h, wdt_ref[...])
    dt_ref[...] = dt

    def project(w_ref, o_ref, cols, scale=None):
        def run():
            y = _dot(h, w_ref[:, cols])
            o_ref[:, cols] = (y if scale is None else y * scale).astype(BF16)
        return run

    tiles = lambda width: [slice(t * MXU_TILE, (t + 1) * MXU_TILE) for t in range(width // MXU_TILE)]
    fillers = ([project(wz_ref, z_ref, cols) for cols in tiles(D_SSM)]
               + [project(wq_ref, q_ref, cols, Q_SCALE) for cols in tiles(D_ATTN)]
               + [project(wk_ref, k_ref, cols) for cols in tiles(D_KV)]
               + [project(wv_ref, v_ref, cols) for cols in tiles(D_KV)])

    def fill(count=1):
        for _ in range(count):
            if fillers:
                fillers.pop(0)()

    first_tap = HALO - (CONV_K - 1) // 2
    n_pieces = CONV_DIM // CONV_PIECE

    def project_conv_input(piece):
        cols = slice(piece * CONV_PIECE, (piece + 1) * CONV_PIECE)
        xbc = _dot(h_wide, wx_ref[:, cols])
        pad_ref[0:HALO, cols] = jnp.where(j < last, xbc[0:HALO, :], 0.0)
        pad_ref[HALO:HALO + SEQ_TILE, cols] = xbc[HALO:HALO + SEQ_TILE, :]
        pad_ref[HALO + SEQ_TILE:2 * HALO + SEQ_TILE, cols] = jnp.where(
            j > 0, xbc[HALO + SEQ_TILE:2 * HALO + SEQ_TILE, :], 0.0)

    def conv(piece):
        cols = slice(piece * CONV_PIECE, (piece + 1) * CONV_PIECE)
        for c in range(CHUNKS_PER_STEP):
            acc = cb_ref[:, cols]
            for k in range(CONV_K):
                start = c * CHUNK + first_tap + k
                acc = acc + pad_ref[start:start + CHUNK, cols] * cw_ref[k:k + 1, cols]
            u_ref[c * CHUNK:(c + 1) * CHUNK, cols] = _silu(acc).astype(BF16)

    project_conv_input(0)
    for piece in range(n_pieces):
        if piece + 1 < n_pieces:
            project_conv_input(piece + 1)
        conv(piece)
        fill()

    tri_u = _tri(lower=False)
    a_row = _neg_exp_row(a_ref)
    chunk_rows = [slice(c * CHUNK, (c + 1) * CHUNK) for c in range(CHUNKS_PER_STEP)]
    dtvs = [_softplus(dt[r, :] + bias_ref[...]) for r in chunk_rows]
    rcss = [_tri_matmul(tri_u, dtv * a_row) for dtv in dtvs]
    fill()
    w_exps = [_expand_heads(jnp.exp(rcs[0:1, :] - rcs) * dtv, e2b_ref) for rcs, dtv in zip(rcss, dtvs)]
    decs = [_expand_heads(jnp.broadcast_to(jnp.exp(rcs[0:1, :]), (SUBLANES, LANES)), e2b_ref)[0:1, :]
            for rcs in rcss]
    fill()
    contribs = []
    for r, w_exp in zip(chunk_rows, w_exps):
        xsw = (u_ref[r, 0:D_SSM].astype(F32) * w_exp).astype(BF16)
        contribs.append([_dot_tn(u_ref[r, D_SSM + g * D_STATE:D_SSM + (g + 1) * D_STATE],
                                 xsw[:, g * GROUP_W:(g + 1) * GROUP_W]) for g in range(SSM_GROUPS)])
        fill()
    fill(len(fillers))
    for c in reversed(range(CHUNKS_PER_STEP)):
        prev_ref[c] = st_ref[...].astype(BF16)
        for g in range(SSM_GROUPS):
            cols = slice(g * GROUP_W, (g + 1) * GROUP_W)
            st_ref[:, cols] = st_ref[:, cols] * decs[c][:, cols] + contribs[c][g]


def _mix_in(x1, nw, wz, wx, wdt, wq, wk, wv, cw, cb, bias, a_log, e2b):
    b, s, _ = x1.shape
    steps = s // SEQ_TILE
    halo_per_tile = SEQ_TILE // HALO
    n_halo = s // HALO
    widths = (D_SSM, CONV_DIM, DT_PAD, D_ATTN, D_KV, D_KV)
    dtypes = (BF16, BF16, F32, BF16, BF16, BF16)
    rev = lambda bi, j: (bi, steps - 1 - j, 0)
    prv = lambda bi, j: (bi, jnp.maximum((steps - 1 - j) * halo_per_tile - 1, 0), 0)
    nxt = lambda bi, j: (bi, jnp.minimum((steps - j) * halo_per_tile, n_halo - 1), 0)
    resident = (2 * D_MODEL * sum(widths) + 2 * 2 * LANES * D_SSM
                + 2 * 4 * SEQ_TILE * (D_MODEL + sum(widths)) + 2 * 2 * SEQ_TILE * D_SSM
                + 4 * (SEQ_TILE + 2 * HALO) * CONV_DIM + 4 * D_STATE * D_SSM)
    return pl.pallas_call(
        _mix_in_kernel,
        grid=(b, steps),
        in_specs=[pl.BlockSpec((None, SEQ_TILE, D_MODEL), rev),
                  pl.BlockSpec((None, HALO, D_MODEL), prv),
                  pl.BlockSpec((None, HALO, D_MODEL), nxt),
                  _resident((1, D_MODEL))] + [_resident((D_MODEL, w)) for w in widths]
                 + [_resident((SUBLANES, CONV_DIM)), _resident((1, CONV_DIM)),
                    _resident((1, DT_PAD)), _resident((1, DT_PAD)), _resident((2 * LANES, D_SSM))],
        out_specs=[pl.BlockSpec((None, SEQ_TILE, w), rev) for w in widths]
                  + [pl.BlockSpec((None, CHUNKS_PER_STEP, D_STATE, D_SSM),
                                  lambda bi, j: (bi, steps - 1 - j, 0, 0))],
        out_shape=[jax.ShapeDtypeStruct((b, s, w), d) for w, d in zip(widths, dtypes)]
                  + [jax.ShapeDtypeStruct((b, s // CHUNK, D_STATE, D_SSM), BF16)],
        scratch_shapes=[pltpu.VMEM((SEQ_TILE + 2 * HALO, CONV_DIM), F32),
                        pltpu.VMEM((D_STATE, D_SSM), F32)],
        compiler_params=pltpu.CompilerParams(
            dimension_semantics=("arbitrary", "arbitrary"), vmem_limit_bytes=_vmem_limit(resident)),
        name="mix_in",
    )(x1, x1, x1, nw, wz, wx, wdt, wq, wk, wv, cw, cb, bias, a_log, e2b)


def _ssd_prologue(c, u_ref, dt_ref, bias_ref, a_row, tri_both, col):
    rows = slice(c * CHUNK, (c + 1) * CHUNK)
    dtv = _softplus(dt_ref[rows, :] + bias_ref[...])
    both = _tri_matmul(tri_both, dtv * a_row)
    cs = both[0:CHUNK, :]
    rcs = both[CHUNK:2 * CHUNK, :]
    comb = jnp.where(col < SSM_HEADS, cs, rcs)
    comb_t = comb.T
    dt_t = dtv.T

    xs = u_ref[rows, 0:D_SSM].astype(F32)
    bmat = [u_ref[rows, D_SSM + g * D_STATE:D_SSM + (g + 1) * D_STATE] for g in range(SSM_GROUPS)]
    cmat = [u_ref[rows, D_SSM + (SSM_GROUPS + g) * D_STATE:D_SSM + (SSM_GROUPS + g + 1) * D_STATE]
            for g in range(SSM_GROUPS)]
    cb = [_dot_nt(cmat[g], bmat[g]) for g in range(SSM_GROUPS)]
    return dict(dtv=dtv, comb=comb, comb_t=comb_t, dt_t=dt_t, xs=xs, bmat=bmat, cmat=cmat, cb=cb)


def _ssd_head_pair(pro, j, row, col):
    half = SSM_HEAD_DIM
    comb, comb_t, dt_t, dtv = pro["comb"], pro["comb_t"], pro["dt_t"], pro["dtv"]
    g = (2 * j) // HEADS_PER_GROUP
    lane_bcast = lambda a, i: jnp.broadcast_to(a[:, i:i + 1], (CHUNK, CHUNK))
    ms, wf, ef, eb, decf = [], [], [], [], []
    for h in (2 * j, 2 * j + 1):
        csf = lane_bcast(comb, h)
        rcsb = lane_bcast(comb, SSM_HEADS + h)
        df = csf - comb_t[h:h + 1, :]
        db = rcsb - comb_t[SSM_HEADS + h:SSM_HEADS + h + 1, :]
        dtf = dt_t[h:h + 1, :]
        dtb = dt_t[SSM_HEADS + h:SSM_HEADS + h + 1, :]
        dsel = jnp.where(row > col, dtf, jnp.where(row < col, dtb, dtf + dtb))
        decay = jnp.exp(jnp.where(row >= col, df, db))
        ms.append((pro["cb"][g] * decay * dsel).astype(BF16))
        last = csf[CHUNK - 1:CHUNK, :]
        wf.append(jnp.exp(last - csf) * lane_bcast(dtv, h))
        ef.append(jnp.exp(csf))
        eb.append(jnp.exp(rcsb))
        decf.append(jnp.exp(last))
    pair = lambda ab: jnp.where(col[0:ab[0].shape[0], :] < half, ab[0], ab[1])
    xt = pro["xs"][:, j * LANES:(j + 1) * LANES]
    rhs = jnp.concatenate([jnp.where(col < half, xt, 0.0), jnp.where(col >= half, xt, 0.0)],
                          axis=0).astype(BF16)
    ydiag = _dot(jnp.concatenate(ms, axis=1), rhs)
    return dict(ydiag=ydiag, wf=pair(wf), ef=pair(ef), eb=pair(eb), decf=pair(decf))


def _ssd_epilogue(c, pro, pairs, z_ref, prev_ref, dskip_ref, nw_ref, y_ref, st_ref):
    rows = slice(c * CHUNK, (c + 1) * CHUNK)
    xs, bmat, cmat = (pro[n] for n in ("xs", "bmat", "cmat"))
    ydiag, wf, ef, eb, decf = (jnp.concatenate([p[n] for p in pairs], axis=1)
                               for n in ("ydiag", "wf", "ef", "eb", "decf"))
    xsw = (xs * wf).astype(BF16)

    for g in range(SSM_GROUPS):
        cols = slice(g * GROUP_W, (g + 1) * GROUP_W)
        state = st_ref[:, cols]
        yoff_f = _dot(cmat[g], state.astype(BF16))
        yoff_b = _dot(cmat[g], prev_ref[c, :, cols])
        y = (ydiag[:, cols] + ef[:, cols] * yoff_f + eb[:, cols] * yoff_b
             + xs[:, cols] * dskip_ref[:, cols])
        y = y * _silu(z_ref[rows, cols].astype(F32))
        y = y * lax.rsqrt(jnp.mean(y * y, axis=-1, keepdims=True) + EPS) * nw_ref[:, cols]
        y_ref[rows, cols] = y.astype(BF16)
        st_ref[:, cols] = state * decf[:, cols] + _dot_tn(bmat[g], xsw[:, cols])


ATTN_KEYS = 3 * CHUNK
Q_TILES_PER_KV_TILE = (D_ATTN // LANES) // (D_KV // LANES)


def _attn_prologue(c, n_chunks, blk, last_blk, k_refs, v_refs):
    keys = ATTN_KEYS
    half = HEAD_DIM
    rows = slice(c * CHUNK, (c + 1) * CHUNK)

    srow = lax.broadcasted_iota(jnp.int32, (keys, CHUNK), 0)
    tcol = lax.broadcasted_iota(jnp.int32, (keys, CHUNK), 1)
    idist = jnp.abs(tcol + CHUNK - srow)
    lo_key = jnp.where(blk == 0, CHUNK, 0)
    hi_key = jnp.where(blk == last_blk, 2 * CHUNK, keys)
    valid = (idist <= WINDOW) & (srow >= lo_key) & (srow < hi_key)
    neg_dist = jnp.where(valid, -idist.astype(F32), -jnp.inf)
    klane = lax.broadcasted_iota(jnp.int32, (keys, LANES), 1)
    vrow = lax.broadcasted_iota(jnp.int32, (LANES, keys), 0)
    srow16 = lax.broadcasted_iota(jnp.int32, (2 * SUBLANES, 2 * keys), 0)
    scol16 = lax.broadcasted_iota(jnp.int32, (2 * SUBLANES, 2 * keys), 1)
    sum_rows = jnp.where((srow16 < SUBLANES) == (scol16 < keys), 1.0, 0.0).astype(BF16)

    def window(refs, lanes):
        main_ref, prev_ref, next_ref = refs
        prev = prev_ref[:, lanes] if c == 0 else main_ref[(c - 1) * CHUNK:c * CHUNK, lanes]
        nxt = (next_ref[:, lanes] if c == n_chunks - 1
               else main_ref[(c + 1) * CHUNK:(c + 2) * CHUNK, lanes])
        return jnp.concatenate([prev, main_ref[rows, lanes], nxt], axis=0)

    kk, vv = [], []
    for m in range(D_KV // LANES):
        lanes = slice(m * LANES, (m + 1) * LANES)
        kf = window(k_refs, lanes).astype(F32)
        kk.append(jnp.concatenate([jnp.where(klane < half, kf, 0.0), jnp.where(klane >= half, kf, 0.0)],
                                  axis=0).astype(BF16))
        vt_t = window(v_refs, lanes).astype(F32).T
        vm = jnp.concatenate([jnp.where(vrow < half, vt_t, 0.0), jnp.where(vrow >= half, vt_t, 0.0)],
                             axis=1).astype(BF16)
        vv.append(jnp.concatenate([vm, sum_rows], axis=0))
    return dict(neg_dist=neg_dist, kk=kk, vv=vv)


Q_TILES_PER_CALL = MXU_TILE // LANES
assert Q_TILES_PER_KV_TILE % Q_TILES_PER_CALL == 0


def _attn_scores(c, apro, i, q_ref):
    rows = slice(c * CHUNK, (c + 1) * CHUNK)
    kv_tile = i * Q_TILES_PER_CALL // Q_TILES_PER_KV_TILE
    q_stack = jnp.concatenate([q_ref[rows, j * LANES:(j + 1) * LANES]
                               for j in range(Q_TILES_PER_CALL * i, Q_TILES_PER_CALL * (i + 1))], axis=0)
    return _dot_nt(apro["kk"][kv_tile], q_stack)


def _attn_finish(c, apro, i, st, slope_ref, sink_ref, o_ref):
    keys = ATTN_KEYS
    half = HEAD_DIM
    rows = slice(c * CHUNK, (c + 1) * CHUNK)
    kv_tile = i * Q_TILES_PER_CALL // Q_TILES_PER_KV_TILE
    orow = lax.broadcasted_iota(jnp.int32, (LANES, CHUNK), 0)
    ps, mxs = [], []
    for t in range(Q_TILES_PER_CALL):
        p_tile = []
        for e in range(2):
            slot = 2 * (Q_TILES_PER_CALL * i + t) + e
            s = st[e * keys:(e + 1) * keys, t * CHUNK:(t + 1) * CHUNK] + slope_ref[slot] * apro["neg_dist"]
            mx = jnp.maximum(jnp.max(s, axis=0, keepdims=True), sink_ref[slot] * LOG2E)
            p_tile.append(jnp.exp2(s - mx).astype(BF16))
            mxs.append(mx)
        ps.append(jnp.concatenate(p_tile, axis=0))
    ot = _dot(apro["vv"][kv_tile], jnp.concatenate(ps, axis=1))
    for t in range(Q_TILES_PER_CALL):
        j = Q_TILES_PER_CALL * i + t
        ot_t = ot[:, t * CHUNK:(t + 1) * CHUNK]
        inv = [1.0 / (ot_t[LANES + e * SUBLANES:LANES + e * SUBLANES + 1, :]
                      + jnp.exp2(sink_ref[2 * j + e] * LOG2E - mxs[2 * t + e])) for e in range(2)]
        out = ot_t[0:LANES, :] * jnp.where(orow < half, inv[0], inv[1])
        o_ref[rows, j * LANES:(j + 1) * LANES] = out.T.astype(BF16)


def _out_ffn2_slices(ys, ya, x_ref, wos_ref, woa_ref, nw2_ref, wg_ref, wu_ref, wd_ref, nwf_ref, o_ref,
                     x2_ref, xn_ref, acc_ref):
    v = {}
    tile = lambda t: slice(t * MXU_TILE, (t + 1) * MXU_TILE)

    def out_proj(y_ref, w_ref, t, first):
        def run():
            base = x_ref[...] if first else x2_ref[...]
            x2_ref[...] = base + _dot(y_ref[:, tile(t)], w_ref[tile(t), :])
        return run

    def gate_up(t):
        def run():
            if t == 0:
                xn_ref[...] = _rmsnorm(x2_ref[...], nw2_ref[...]).astype(BF16)
            v["g", t] = _dot(xn_ref[...], wg_ref[:, tile(t)])
            v["u", t] = _dot(xn_ref[...], wu_ref[:, tile(t)])
        return run

    def down(t):
        def run():
            hid = (_silu(v.pop(("g", t))) * v.pop(("u", t))).astype(BF16)
            part = _dot(hid, wd_ref[tile(t), :])
            acc_ref[...] = part if t == 0 else acc_ref[...] + part
        return run

    def finish():
        o_ref[...] = _rmsnorm(x2_ref[...] + 0.5 * acc_ref[...], nwf_ref[...])

    n_y = D_SSM // MXU_TILE
    n_ff = D_FF // MXU_TILE
    items = ([out_proj(ys, wos_ref, t, t == 0) for t in range(n_y)]
             + [out_proj(ya, woa_ref, t, False) for t in range(D_ATTN // MXU_TILE)])
    for t in range(n_ff):
        items.append(gate_up(t))
        if t > 0:
            items.append(down(t - 1))
    items += [down(n_ff - 1), finish]
    return items


def _mix_out_kernel(slope_ref, sink_ref, u_ref, dt_ref, z_ref, prev_ref, q_ref,
                    km_ref, kp_ref, kn_ref, vm_ref, vp_ref, vn_ref, x_ref,
                    bias_ref, a_ref, dskip_ref, nw_ref,
                    wos_ref, woa_ref, nw2_ref, wg_ref, wu_ref, wd_ref, nwf_ref,
                    o_ref, ys_ref, ya_ref, st_ref, yps_ref, ypa_ref, x2_ref, xn_ref, acc_ref):
    j = pl.program_id(1)
    n_tiles = pl.num_programs(1) - 1

    @pl.when(j == 0)
    def _():
        st_ref[...] = jnp.zeros_like(st_ref)

    def output_stage_slices():
        yps_ref[...] = ys_ref[...]
        ypa_ref[...] = ya_ref[...]
        return _out_ffn2_slices(yps_ref, ypa_ref, x_ref, wos_ref, woa_ref, nw2_ref,
                                wg_ref, wu_ref, wd_ref, nwf_ref, o_ref, x2_ref, xn_ref, acc_ref)

    def mixers(fillers):
        def fill(count=1):
            for _ in range(count):
                if fillers:
                    fillers.pop(0)()

        tri_both = jnp.concatenate([_tri(lower=True), _tri(lower=False)], axis=0)
        row = lax.broadcasted_iota(jnp.int32, (CHUNK, CHUNK), 0)
        col = lax.broadcasted_iota(jnp.int32, (CHUNK, CHUNK), 1)
        a_row = _neg_exp_row(a_ref)
        n_pairs = SSM_HEADS // 2
        assert n_pairs == D_ATTN // LANES
        fill(2)
        for c in range(MIX_CHUNKS):
            pro = _ssd_prologue(c, u_ref, dt_ref, bias_ref, a_row, tri_both, col)
            fill()
            apro = _attn_prologue(c, MIX_CHUNKS, j * MIX_CHUNKS + c, n_tiles * MIX_CHUNKS - 1,
                                  (km_ref, kp_ref, kn_ref), (vm_ref, vp_ref, vn_ref))
            fill()
            ydiag = []
            n_calls = n_pairs // Q_TILES_PER_CALL
            st_next = _attn_scores(c, apro, 0, q_ref)
            for i in range(n_calls):
                st = st_next
                if i + 1 < n_calls:
                    st_next = _attn_scores(c, apro, i + 1, q_ref)
                fill()
                for t in range(Q_TILES_PER_CALL):
                    ydiag.append(_ssd_head_pair(pro, Q_TILES_PER_CALL * i + t, row, col))
                    fill()
                _attn_finish(c, apro, i, st, slope_ref, sink_ref, ya_ref)
            fill(1 if c + 1 < MIX_CHUNKS else len(fillers))
            _ssd_epilogue(c, pro, ydiag, z_ref, prev_ref, dskip_ref, nw_ref, ys_ref, st_ref)
        assert not fillers

    @pl.when(j == 0)
    def _():
        mixers([])

    @pl.when(jnp.logical_and(j > 0, j < n_tiles))
    def _():
        mixers(output_stage_slices())

    @pl.when(j == n_tiles)
    def _():
        for run in output_stage_slices():
            run()


def _mix_out(u, dt, z, prevb, q, k, v, x1, slopes2, sink, bias, a_log, dskip, nw,
             wos, woa, nw2, wg, wu, wd, nwf):
    b, s, _ = u.shape
    n_tiles = s // MIX_TILE
    nblk = s // CHUNK
    cur = lambda j: jnp.minimum(j, n_tiles - 1)
    mix = lambda bi, j: (bi, cur(j), 0)
    lag = lambda bi, j: (bi, jnp.maximum(j - 1, 0), 0)
    prv = lambda bi, j: (bi, jnp.maximum(cur(j) * MIX_CHUNKS - 1, 0), 0)
    nxt = lambda bi, j: (bi, jnp.minimum((cur(j) + 1) * MIX_CHUNKS, nblk - 1), 0)
    smem = pl.BlockSpec(memory_space=pltpu.SMEM)
    kv_main = pl.BlockSpec((None, MIX_TILE, D_KV), mix)
    kv_prev = pl.BlockSpec((None, CHUNK, D_KV), prv)
    kv_next = pl.BlockSpec((None, CHUNK, D_KV), nxt)
    resident = (2 * (3 * D_MODEL * D_FF + (D_SSM + D_ATTN) * D_MODEL)
                + 2 * 2 * MIX_TILE * (CONV_DIM + 2 * D_SSM + D_ATTN + 2 * D_KV) + 2 * 4 * MIX_TILE * DT_PAD
                + 2 * 2 * 4 * MIX_TILE * D_MODEL + 2 * MIX_TILE * (D_SSM + D_ATTN) + 4 * D_STATE * D_SSM)
    return pl.pallas_call(
        _mix_out_kernel,
        grid=(b, n_tiles + 1),
        in_specs=[smem, smem,
                  pl.BlockSpec((None, MIX_TILE, CONV_DIM), mix),
                  pl.BlockSpec((None, MIX_TILE, DT_PAD), mix),
                  pl.BlockSpec((None, MIX_TILE, D_SSM), mix),
                  pl.BlockSpec((None, MIX_CHUNKS, D_STATE, D_SSM), lambda bi, j: (bi, cur(j), 0, 0)),
                  pl.BlockSpec((None, MIX_TILE, D_ATTN), mix),
                  kv_main, kv_prev, kv_next, kv_main, kv_prev, kv_next,
                  pl.BlockSpec((None, MIX_TILE, D_MODEL), lag),
                  _resident((1, DT_PAD)), _resident((1, DT_PAD)),
                  _resident((1, D_SSM)), _resident((1, D_SSM)),
                  _resident((D_SSM, D_MODEL)), _resident((D_ATTN, D_MODEL)), _resident((1, D_MODEL)),
                  _resident((D_MODEL, D_FF)), _resident((D_MODEL, D_FF)), _resident((D_FF, D_MODEL)),
                  _resident((1, D_MODEL))],
        out_specs=pl.BlockSpec((None, MIX_TILE, D_MODEL), lag),
        out_shape=jax.ShapeDtypeStruct((b, s, D_MODEL), F32),
        scratch_shapes=[pltpu.VMEM((MIX_TILE, D_SSM), BF16), pltpu.VMEM((MIX_TILE, D_ATTN), BF16),
                        pltpu.VMEM((D_STATE, D_SSM), F32),
                        pltpu.VMEM((MIX_TILE, D_SSM), BF16), pltpu.VMEM((MIX_TILE, D_ATTN), BF16),
                        pltpu.VMEM((MIX_TILE, D_MODEL), F32), pltpu.VMEM((MIX_TILE, D_MODEL), BF16),
                        pltpu.VMEM((MIX_TILE, D_MODEL), F32)],
        compiler_params=pltpu.CompilerParams(
            dimension_semantics=("arbitrary", "arbitrary"), vmem_limit_bytes=_vmem_limit(resident)),
        name="mix_out",
    )(slopes2, sink, u, dt, z, prevb, q, k, k, k, v, v, v, x1, bias, a_log, dskip, nw,
      wos, woa, nw2, wg, wu, wd, nwf)


def _head_selection(offset):
    e = np.zeros((2 * LANES, D_SSM), np.float32)
    for h in range(SSM_HEADS):
        e[offset + h, h * SSM_HEAD_DIM:(h + 1) * SSM_HEAD_DIM] = 1.0
        e[LANES + offset + h, h * SSM_HEAD_DIM:(h + 1) * SSM_HEAD_DIM] = 1.0
    return jnp.asarray(e, BF16)


def _prepare(norm_ffn1_w, ffn1_w_gate, ffn1_w_up, ffn1_w_down, norm_mix_w, w_in, conv_w, conv_b,
             dt_bias_fwd, dt_bias_bwd, a_log_fwd, a_log_bwd, d_skip, ssm_norm_w, attn_sink, w_out,
             norm_ffn2_w, ffn2_w_gate, ffn2_w_up, ffn2_w_down, norm_final_w):
    row = lambda v: v.reshape(1, -1).astype(F32)
    splits = np.cumsum((D_SSM, CONV_DIM, SSM_HEADS, SSM_HEADS, D_ATTN, D_KV))
    wz, wx, wdtf, wdtb, wq, wk, wv = jnp.split(w_in, splits, axis=1)
    wdt = jnp.concatenate([wdtf, wdtb, jnp.zeros((D_MODEL, DT_PAD - 2 * SSM_HEADS), F32)], axis=1)
    order = np.asarray(ATTN_HEAD_ORDER)
    cols = (order[:, None] * HEAD_DIM + np.arange(HEAD_DIM)[None, :]).reshape(-1)
    wq = wq[:, cols]
    pad_dt = jnp.zeros((DT_PAD - 2 * SSM_HEADS,), F32)
    slopes = jnp.exp2(-(8.0 / N_HEADS) * jnp.arange(1, N_HEADS + 1, dtype=F32))
    return dict(
        nw1=row(norm_ffn1_w), wg1=ffn1_w_gate.astype(BF16), wu1=ffn1_w_up.astype(BF16),
        wd1=ffn1_w_down.astype(BF16),
        nwm=row(norm_mix_w), wz=wz.astype(BF16), wx=wx.astype(BF16), wdt=wdt.astype(BF16),
        wq=wq.astype(BF16), wk=wk.astype(BF16), wv=wv.astype(BF16),
        cw=jnp.concatenate([conv_w, jnp.zeros((SUBLANES - CONV_K, CONV_DIM), F32)], axis=0),
        cb=row(conv_b),
        dt_bias=row(jnp.concatenate([dt_bias_fwd, dt_bias_bwd, pad_dt])),
        a_log=row(jnp.concatenate([a_log_fwd, a_log_bwd, pad_dt])),
        e2b=_head_selection(SSM_HEADS),
        dskip=row(jnp.repeat(d_skip, SSM_HEAD_DIM)), nws=row(ssm_norm_w),
        slopes2=(slopes * LOG2E)[order], sink=attn_sink.astype(F32)[order],
        wos=w_out[:D_SSM].astype(BF16), woa=w_out[D_SSM:][cols].astype(BF16),
        nw2=row(norm_ffn2_w), wg2=ffn2_w_gate.astype(BF16), wu2=ffn2_w_up.astype(BF16),
        wd2=ffn2_w_down.astype(BF16), nwf=row(norm_final_w),
    )


def _trunk(x, p):
    b, s, _ = x.shape
    assert s % SEQ_TILE == 0 and (b * s) % TOK_TILE == 0 and x.shape[2] == D_MODEL
    n = b * s
    x1 = _ffn1(x.reshape(n, D_MODEL), p["nw1"], p["wg1"], p["wu1"], p["wd1"])
    z, u, dt, q, k, v, prevb = _mix_in(
        x1.reshape(b, s, D_MODEL), p["nwm"], p["wz"], p["wx"], p["wdt"], p["wq"], p["wk"], p["wv"],
        p["cw"], p["cb"], p["dt_bias"], p["a_log"], p["e2b"])
    return _mix_out(u, dt, z, prevb, q, k, v, x1.reshape(b, s, D_MODEL), p["slopes2"], p["sink"],
                    p["dt_bias"], p["a_log"], p["dskip"], p["nws"],
                    p["wos"], p["woa"], p["nw2"], p["wg2"], p["wu2"], p["wd2"], p["nwf"])


def kernel(x_prompt, x_sample, norm_ffn1_w, ffn1_w_gate, ffn1_w_up, ffn1_w_down, norm_mix_w, w_in, conv_w, conv_b, dt_bias_fwd, dt_bias_bwd, a_log_fwd, a_log_bwd, d_skip, ssm_norm_w, attn_sink, w_out, norm_ffn2_w, ffn2_w_gate, ffn2_w_up, ffn2_w_down, norm_final_w):
    p = _prepare(norm_ffn1_w[0], ffn1_w_gate[0], ffn1_w_up[0], ffn1_w_down[0], norm_mix_w[0], w_in[0],
                 conv_w[0], conv_b[0], dt_bias_fwd[0], dt_bias_bwd[0], a_log_fwd[0], a_log_bwd[0],
                 d_skip[0], ssm_norm_w[0], attn_sink[0], w_out[0], norm_ffn2_w[0], ffn2_w_gate[0],
                 ffn2_w_up[0], ffn2_w_down[0], norm_final_w)
    return (_trunk(x_prompt, p), _trunk(x_sample, p))
```

```python
import math

import jax
import jax.numpy as jnp
import numpy as np
from jax import lax
from jax.experimental import pallas as pl
from jax.experimental.pallas import tpu as pltpu

F32 = jnp.float32
BF16 = jnp.bfloat16

D_MODEL = 1024
D_FF = 2816
SSM_HEADS = 16
SSM_HEAD_DIM = 64
D_SSM = SSM_HEADS * SSM_HEAD_DIM
SSM_GROUPS = 2
D_STATE = 128
CONV_K = 5
CONV_DIM = D_SSM + 2 * SSM_GROUPS * D_STATE
N_HEADS = 16
KV_HEADS = 4
HEAD_DIM = 64
D_ATTN = N_HEADS * HEAD_DIM
D_KV = KV_HEADS * HEAD_DIM
QKV_W = D_ATTN + 2 * D_KV
WINDOW = 128
EPS = 1e-6
LOG2E = math.log2(math.e)
Q_SCALE = LOG2E / math.sqrt(HEAD_DIM)

LANES = 128
SUBLANES = 8
VMEM_BYTES_V7X = 64 * 1024 * 1024

CHUNK = LANES
TOK_TILE = 1024
SEQ_TILE = 512
CHUNKS_PER_STEP = SEQ_TILE // CHUNK
MIX_TILE = 256
MIX_CHUNKS = MIX_TILE // CHUNK
MXU_TILE = 256
FF_SPLIT = (D_FF // MXU_TILE + 1) // 2 * MXU_TILE
FF_CHUNK = FF_SPLIT
HALO = 2 * SUBLANES
CONV_PIECE = MXU_TILE
DT_PAD = LANES
HEADS_PER_GROUP = SSM_HEADS // SSM_GROUPS
GROUP_W = HEADS_PER_GROUP * SSM_HEAD_DIM

ATTN_HEAD_ORDER = (0, 4, 1, 5, 2, 6, 3, 7, 8, 12, 9, 13, 10, 14, 11, 15)


def _vmem_limit(resident_bytes):
    return int(min(VMEM_BYTES_V7X - 8 * 1024 * 1024, 2 * resident_bytes + 8 * 1024 * 1024))


def _resident(shape):
    nd = len(shape)
    return pl.BlockSpec(shape, lambda *_: (0,) * nd, pipeline_mode=pl.Buffered(1))


def _dot(a, b):
    return jnp.dot(a, b, preferred_element_type=F32)


def _dot_nt(a, b):
    return lax.dot_general(a, b, (((1,), (1,)), ((), ())), preferred_element_type=F32)


def _dot_tn(a, b):
    return lax.dot_general(a, b, (((0,), (0,)), ((), ())), preferred_element_type=F32)


def _rmsnorm(x, w):
    return x * lax.rsqrt(jnp.mean(x * x, axis=-1, keepdims=True) + EPS) * w


def _silu(x):
    return x * jax.nn.sigmoid(x)


def _softplus(x):
    return jnp.maximum(x, 0.0) + jnp.log1p(jnp.exp(-jnp.abs(x)))


def _split2(x):
    hi = x.astype(BF16)
    lo = (x - hi.astype(F32)).astype(BF16)
    return hi, lo


def _split3(x):
    hi = x.astype(BF16)
    r = x - hi.astype(F32)
    mid = r.astype(BF16)
    lo = (r - mid.astype(F32)).astype(BF16)
    return hi, mid, lo


def _swiglu(xn, wg_ref, wu_ref, wd_ref):
    acc = None
    for cols in (slice(0, FF_SPLIT), slice(FF_SPLIT, D_FF)):
        g = _dot(xn, wg_ref[:, cols])
        u = _dot(xn, wu_ref[:, cols])
        part = _dot((_silu(g) * u).astype(BF16), wd_ref[cols, :])
        acc = part if acc is None else acc + part
    return acc


def _ffn1_kernel(x_ref, nw_ref, wg_ref, wu_ref, wd_ref, o_ref):
    x = x_ref[...]
    xn = _rmsnorm(x, nw_ref[...]).astype(BF16)
    o_ref[...] = x + 0.5 * _swiglu(xn, wg_ref, wu_ref, wd_ref)


def _ffn1(x2d, nw, wg, wu, wd):
    n = x2d.shape[0]
    tile = pl.BlockSpec((TOK_TILE, D_MODEL), lambda i: (i, 0))
    resident = 2 * 3 * D_MODEL * D_FF + 4 * 4 * TOK_TILE * D_MODEL + 3 * 4 * TOK_TILE * FF_CHUNK
    return pl.pallas_call(
        _ffn1_kernel,
        grid=(n // TOK_TILE,),
        in_specs=[tile, _resident((1, D_MODEL)), _resident((D_MODEL, D_FF)),
                  _resident((D_MODEL, D_FF)), _resident((D_FF, D_MODEL))],
        out_specs=tile,
        out_shape=jax.ShapeDtypeStruct((n, D_MODEL), F32),
        compiler_params=pltpu.CompilerParams(
            dimension_semantics=("parallel",), vmem_limit_bytes=_vmem_limit(resident)),
        name="ffn1",
    )(x2d, nw, wg, wu, wd)


def _tri(lower):
    r = lax.broadcasted_iota(jnp.int32, (CHUNK, CHUNK), 0)
    c = lax.broadcasted_iota(jnp.int32, (CHUNK, CHUNK), 1)
    return jnp.where((r >= c) if lower else (r <= c), 1.0, 0.0).astype(BF16)


def _tri_matmul(tri, x):
    hi, mid, lo = _split3(x)
    out = _dot(tri, jnp.concatenate([hi, mid, lo], axis=1))
    w = x.shape[1]
    return out[:, 0:w] + out[:, w:2 * w] + out[:, 2 * w:3 * w]


def _neg_exp_row(a_log_ref):
    lane = lax.broadcasted_iota(jnp.int32, (1, DT_PAD), 1)
    return jnp.where(lane < 2 * SSM_HEADS, -jnp.exp(a_log_ref[...]), 0.0)


def _expand_heads(x, e2_ref):
    hi, lo = _split2(x)
    return _dot(jnp.concatenate([hi, lo], axis=1), e2_ref[...])


def _mix_in_kernel(xm_ref, xp_ref, xn_ref, nw_ref, wz_ref, wx_ref, wdt_ref, wq_ref, wkv_ref,
                   cw_ref, cb_ref, bias_ref, a_ref, e2b_ref,
                   z_ref, u_ref, dt_ref, qkv_ref, prev_ref, pad_ref, st_ref):
    j = pl.program_id(1)
    last = pl.num_programs(1) - 1

    @pl.when(j == 0)
    def _():
        st_ref[...] = jnp.zeros_like(st_ref)

    h = _rmsnorm(xm_ref[...], nw_ref[...]).astype(BF16)
    h_wide = jnp.concatenate([_rmsnorm(xp_ref[...], nw_ref[...]).astype(BF16), h,
                              _rmsnorm(xn_ref[...], nw_ref[...]).astype(BF16)], axis=0)
    dt = _dot(h, wdt_ref[...])
    dt_ref[...] = dt

    def project(w_ref, o_ref, cols, scale=None, out_offset=0):
        def run():
            y = _dot(h, w_ref[:, cols])
            out_cols = slice(cols.start + out_offset, cols.stop + out_offset)
            o_ref[:, out_cols] = (y if scale is None else y * scale).astype(BF16)
        return run

    tiles = lambda width: [slice(t * MXU_TILE, (t + 1) * MXU_TILE) for t in range(width // MXU_TILE)]
    fillers = ([project(wz_ref, z_ref, cols) for cols in tiles(D_SSM)]
               + [project(wq_ref, qkv_ref, cols, Q_SCALE) for cols in tiles(D_ATTN)]
               + [project(wkv_ref, qkv_ref, cols, None, D_ATTN) for cols in tiles(2 * D_KV)])

    def fill(count=1):
        for _ in range(count):
            if fillers:
                fillers.pop(0)()

    first_tap = HALO - (CONV_K - 1) // 2
    n_pieces = CONV_DIM // CONV_PIECE

    def project_conv_input(piece):
        cols = slice(piece * CONV_PIECE, (piece + 1) * CONV_PIECE)
        xbc = _dot(h_wide, wx_ref[:, cols])
        pad_ref[0:HALO, cols] = jnp.where(j < last, xbc[0:HALO, :], 0.0)
        pad_ref[HALO:HALO + SEQ_TILE, cols] = xbc[HALO:HALO + SEQ_TILE, :]
        pad_ref[HALO + SEQ_TILE:2 * HALO + SEQ_TILE, cols] = jnp.where(
            j > 0, xbc[HALO + SEQ_TILE:2 * HALO + SEQ_TILE, :], 0.0)

    def conv(piece):
        cols = slice(piece * CONV_PIECE, (piece + 1) * CONV_PIECE)
        for c in range(CHUNKS_PER_STEP):
            acc = cb_ref[:, cols]
            for k in range(CONV_K):
                start = c * CHUNK + first_tap + k
                acc = acc + pad_ref[start:start + CHUNK, cols] * cw_ref[k:k + 1, cols]
            u_ref[c * CHUNK:(c + 1) * CHUNK, cols] = _silu(acc).astype(BF16)

    project_conv_input(0)
    for piece in range(n_pieces):
        if piece + 1 < n_pieces:
            project_conv_input(piece + 1)
        conv(piece)
        fill()

    tri_u = _tri(lower=False)
    a_row = _neg_exp_row(a_ref)
    chunk_rows = [slice(c * CHUNK, (c + 1) * CHUNK) for c in range(CHUNKS_PER_STEP)]
    dtvs = [_softplus(dt[r, :] + bias_ref[...]) for r in chunk_rows]
    rcss = [_tri_matmul(tri_u, dtv * a_row) for dtv in dtvs]
    fill()
    w_exps = [_expand_heads(jnp.exp(rcs[0:1, :] - rcs) * dtv, e2b_ref) for rcs, dtv in zip(rcss, dtvs)]
    decs = [_expand_heads(jnp.broadcast_to(jnp.exp(rcs[0:1, :]), (SUBLANES, LANES)), e2b_ref)[0:1, :]
            for rcs in rcss]
    fill()
    contribs = []
    for r, w_exp in zip(chunk_rows, w_exps):
        xsw = (u_ref[r, 0:D_SSM].astype(F32) * w_exp).astype(BF16)
        contribs.append([_dot_tn(u_ref[r, D_SSM + g * D_STATE:D_SSM + (g + 1) * D_STATE],
                                 xsw[:, g * GROUP_W:(g + 1) * GROUP_W]) for g in range(SSM_GROUPS)])
        fill()
    fill(len(fillers))
    for c in reversed(range(CHUNKS_PER_STEP)):
        prev_ref[c] = st_ref[...].astype(BF16)
        for g in range(SSM_GROUPS):
            cols = slice(g * GROUP_W, (g + 1) * GROUP_W)
            st_ref[:, cols] = st_ref[:, cols] * decs[c][:, cols] + contribs[c][g]


def _mix_in(x1, nw, wz, wx, wdt, wq, wkv, cw, cb, bias, a_log, e2b):
    b, s, _ = x1.shape
    steps = s // SEQ_TILE
    halo_per_tile = SEQ_TILE // HALO
    n_halo = s // HALO
    in_widths = (D_SSM, CONV_DIM, DT_PAD, D_ATTN, 2 * D_KV)
    widths = (D_SSM, CONV_DIM, DT_PAD, QKV_W)
    dtypes = (BF16, BF16, F32, BF16)
    rev = lambda bi, j: (bi, steps - 1 - j, 0)
    prv = lambda bi, j: (bi, jnp.maximum((steps - 1 - j) * halo_per_tile - 1, 0), 0)
    nxt = lambda bi, j: (bi, jnp.minimum((steps - j) * halo_per_tile, n_halo - 1), 0)
    resident = (2 * D_MODEL * sum(widths) + 2 * 2 * LANES * D_SSM
                + 2 * 4 * SEQ_TILE * (D_MODEL + sum(widths)) + 2 * 2 * SEQ_TILE * D_SSM
                + 4 * (SEQ_TILE + 2 * HALO) * CONV_DIM + 4 * D_STATE * D_SSM)
    return pl.pallas_call(
        _mix_in_kernel,
        grid=(b, steps),
        in_specs=[pl.BlockSpec((None, SEQ_TILE, D_MODEL), rev),
                  pl.BlockSpec((None, HALO, D_MODEL), prv),
                  pl.BlockSpec((None, HALO, D_MODEL), nxt),
                  _resident((1, D_MODEL))] + [_resident((D_MODEL, w)) for w in in_widths]
                 + [_resident((SUBLANES, CONV_DIM)), _resident((1, CONV_DIM)),
                    _resident((1, DT_PAD)), _resident((1, DT_PAD)), _resident((2 * LANES, D_SSM))],
        out_specs=[pl.BlockSpec((None, SEQ_TILE, w), rev) for w in widths]
                  + [pl.BlockSpec((None, CHUNKS_PER_STEP, D_STATE, D_SSM),
                                  lambda bi, j: (bi, steps - 1 - j, 0, 0))],
        out_shape=[jax.ShapeDtypeStruct((b, s, w), d) for w, d in zip(widths, dtypes)]
                  + [jax.ShapeDtypeStruct((b, s // CHUNK, D_STATE, D_SSM), BF16)],
        scratch_shapes=[pltpu.VMEM((SEQ_TILE + 2 * HALO, CONV_DIM), F32),
                        pltpu.VMEM((D_STATE, D_SSM), F32)],
        compiler_params=pltpu.CompilerParams(
            dimension_semantics=("arbitrary", "arbitrary"), vmem_limit_bytes=_vmem_limit(resident)),
        name="mix_in",
    )(x1, x1, x1, nw, wz, wx, wdt, wq, wkv, cw, cb, bias, a_log, e2b)


def _ssd_prologue(c, u_ref, dt_ref, bias_ref, a_row, tri_both, col):
    rows = slice(c * CHUNK, (c + 1) * CHUNK)
    dtv = _softplus(dt_ref[rows, :] + bias_ref[...])
    both = _tri_matmul(tri_both, dtv * a_row)
    cs = both[0:CHUNK, :]
    rcs = both[CHUNK:2 * CHUNK, :]
    comb = jnp.where(col < SSM_HEADS, cs, rcs)
    comb_t = comb.T
    dt_t = dtv.T

    xs = u_ref[rows, 0:D_SSM].astype(F32)
    bmat = [u_ref[rows, D_SSM + g * D_STATE:D_SSM + (g + 1) * D_STATE] for g in range(SSM_GROUPS)]
    cmat = [u_ref[rows, D_SSM + (SSM_GROUPS + g) * D_STATE:D_SSM + (SSM_GROUPS + g + 1) * D_STATE]
            for g in range(SSM_GROUPS)]
    cb = [_dot_nt(cmat[g], bmat[g]) for g in range(SSM_GROUPS)]
    return dict(dtv=dtv, comb=comb, comb_t=comb_t, dt_t=dt_t, xs=xs, bmat=bmat, cmat=cmat, cb=cb)


def _ssd_head_pair(pro, j, row, col):
    half = SSM_HEAD_DIM
    comb, comb_t, dt_t, dtv = pro["comb"], pro["comb_t"], pro["dt_t"], pro["dtv"]
    g = (2 * j) // HEADS_PER_GROUP
    lane_bcast = lambda a, i: jnp.broadcast_to(a[:, i:i + 1], (CHUNK, CHUNK))
    ms, wf, ef, eb, decf = [], [], [], [], []
    for h in (2 * j, 2 * j + 1):
        csf = lane_bcast(comb, h)
        rcsb = lane_bcast(comb, SSM_HEADS + h)
        df = csf - comb_t[h:h + 1, :]
        db = rcsb - comb_t[SSM_HEADS + h:SSM_HEADS + h + 1, :]
        dtf = dt_t[h:h + 1, :]
        dtb = dt_t[SSM_HEADS + h:SSM_HEADS + h + 1, :]
        dsel = jnp.where(row > col, dtf, jnp.where(row < col, dtb, dtf + dtb))
        decay = jnp.exp(jnp.where(row >= col, df, db))
        ms.append((pro["cb"][g] * decay * dsel).astype(BF16))
        last = csf[CHUNK - 1:CHUNK, :]
        wf.append(jnp.exp(last - csf) * lane_bcast(dtv, h))
        ef.append(jnp.exp(csf))
        eb.append(jnp.exp(rcsb))
        decf.append(jnp.exp(last))
    pair = lambda ab: jnp.where(col[0:ab[0].shape[0], :] < half, ab[0], ab[1])
    xt = pro["xs"][:, j * LANES:(j + 1) * LANES]
    rhs = jnp.concatenate([jnp.where(col < half, xt, 0.0), jnp.where(col >= half, xt, 0.0)],
                          axis=0).astype(BF16)
    ydiag = _dot(jnp.concatenate(ms, axis=1), rhs)
    return dict(ydiag=ydiag, wf=pair(wf), ef=pair(ef), eb=pair(eb), decf=pair(decf))


def _ssd_epilogue(c, pro, pairs, z_ref, prev_ref, dskip_ref, nw_ref, y_ref, st_ref):
    rows = slice(c * CHUNK, (c + 1) * CHUNK)
    xs, bmat, cmat = (pro[n] for n in ("xs", "bmat", "cmat"))
    ydiag, wf, ef, eb, decf = (jnp.concatenate([p[n] for p in pairs], axis=1)
                               for n in ("ydiag", "wf", "ef", "eb", "decf"))
    xsw = (xs * wf).astype(BF16)

    for g in range(SSM_GROUPS):
        cols = slice(g * GROUP_W, (g + 1) * GROUP_W)
        state = st_ref[:, cols]
        yoff_f = _dot(cmat[g], state.astype(BF16))
        yoff_b = _dot(cmat[g], prev_ref[c, :, cols])
        y = (ydiag[:, cols] + ef[:, cols] * yoff_f + eb[:, cols] * yoff_b
             + xs[:, cols] * dskip_ref[:, cols])
        y = y * _silu(z_ref[rows, cols].astype(F32))
        y = y * lax.rsqrt(jnp.mean(y * y, axis=-1, keepdims=True) + EPS) * nw_ref[:, cols]
        y_ref[rows, cols] = y.astype(BF16)
        st_ref[:, cols] = state * decf[:, cols] + _dot_tn(bmat[g], xsw[:, cols])


ATTN_KEYS = 3 * CHUNK
Q_TILES_PER_KV_TILE = (D_ATTN // LANES) // (D_KV // LANES)


def _attn_prologue(c, n_chunks, blk, last_blk, kv_refs):
    keys = ATTN_KEYS
    half = HEAD_DIM
    rows = slice(c * CHUNK, (c + 1) * CHUNK)

    srow = lax.broadcasted_iota(jnp.int32, (keys, CHUNK), 0)
    tcol = lax.broadcasted_iota(jnp.int32, (keys, CHUNK), 1)
    idist = jnp.abs(tcol + CHUNK - srow)
    lo_key = jnp.where(blk == 0, CHUNK, 0)
    hi_key = jnp.where(blk == last_blk, 2 * CHUNK, keys)
    valid = (idist <= WINDOW) & (srow >= lo_key) & (srow < hi_key)
    neg_dist = jnp.where(valid, -idist.astype(F32), -jnp.inf)
    klane = lax.broadcasted_iota(jnp.int32, (keys, LANES), 1)
    vrow = lax.broadcasted_iota(jnp.int32, (LANES, keys), 0)
    srow16 = lax.broadcasted_iota(jnp.int32, (2 * SUBLANES, 2 * keys), 0)
    scol16 = lax.broadcasted_iota(jnp.int32, (2 * SUBLANES, 2 * keys), 1)
    sum_rows = jnp.where((srow16 < SUBLANES) == (scol16 < keys), 1.0, 0.0).astype(BF16)

    def window(first_lane):
        main_ref, prev_ref, next_ref = kv_refs
        halo = slice(first_lane, first_lane + LANES)
        lanes = slice(D_ATTN + first_lane, D_ATTN + first_lane + LANES)
        prev = prev_ref[:, halo] if c == 0 else main_ref[(c - 1) * CHUNK:c * CHUNK, lanes]
        nxt = (next_ref[:, halo] if c == n_chunks - 1
               else main_ref[(c + 1) * CHUNK:(c + 2) * CHUNK, lanes])
        return jnp.concatenate([prev, main_ref[rows, lanes], nxt], axis=0)

    kk, vv = [], []
    for m in range(D_KV // LANES):
        kf = window(m * LANES).astype(F32)
        kk.append(jnp.concatenate([jnp.where(klane < half, kf, 0.0), jnp.where(klane >= half, kf, 0.0)],
                                  axis=0).astype(BF16))
        vt_t = window(D_KV + m * LANES).astype(F32).T
        vm = jnp.concatenate([jnp.where(vrow < half, vt_t, 0.0), jnp.where(vrow >= half, vt_t, 0.0)],
                             axis=1).astype(BF16)
        vv.append(jnp.concatenate([vm, sum_rows], axis=0))
    return dict(neg_dist=neg_dist, kk=kk, vv=vv)


Q_TILES_PER_CALL = MXU_TILE // LANES
assert Q_TILES_PER_KV_TILE % Q_TILES_PER_CALL == 0


def _attn_scores(c, apro, i, q_ref):
    rows = slice(c * CHUNK, (c + 1) * CHUNK)
    kv_tile = i * Q_TILES_PER_CALL // Q_TILES_PER_KV_TILE
    q_stack = jnp.concatenate([q_ref[rows, j * LANES:(j + 1) * LANES]
                               for j in range(Q_TILES_PER_CALL * i, Q_TILES_PER_CALL * (i + 1))], axis=0)
    return _dot_nt(apro["kk"][kv_tile], q_stack)


def _attn_finish(c, apro, i, st, slope_ref, sink_ref, o_ref):
    keys = ATTN_KEYS
    half = HEAD_DIM
    rows = slice(c * CHUNK, (c + 1) * CHUNK)
    kv_tile = i * Q_TILES_PER_CALL // Q_TILES_PER_KV_TILE
    orow = lax.broadcasted_iota(jnp.int32, (LANES, CHUNK), 0)
    ps, mxs = [], []
    for t in range(Q_TILES_PER_CALL):
        p_tile = []
        for e in range(2):
            slot = 2 * (Q_TILES_PER_CALL * i + t) + e
            s = st[e * keys:(e + 1) * keys, t * CHUNK:(t + 1) * CHUNK] + slope_ref[slot] * apro["neg_dist"]
            mx = jnp.maximum(jnp.max(s, axis=0, keepdims=True), sink_ref[slot] * LOG2E)
            p_tile.append(jnp.exp2(s - mx).astype(BF16))
            mxs.append(mx)
        ps.append(jnp.concatenate(p_tile, axis=0))
    ot = _dot(apro["vv"][kv_tile], jnp.concatenate(ps, axis=1))
    for t in range(Q_TILES_PER_CALL):
        j = Q_TILES_PER_CALL * i + t
        ot_t = ot[:, t * CHUNK:(t + 1) * CHUNK]
        inv = [1.0 / (ot_t[LANES + e * SUBLANES:LANES + e * SUBLANES + 1, :]
                      + jnp.exp2(sink_ref[2 * j + e] * LOG2E - mxs[2 * t + e])) for e in range(2)]
        out = ot_t[0:LANES, :] * jnp.where(orow < half, inv[0], inv[1])
        o_ref[rows, j * LANES:(j + 1) * LANES] = out.T.astype(BF16)


def _out_ffn2_slices(ys, ya, x_ref, wos_ref, woa_ref, nw2_ref, wg_ref, wu_ref, wd_ref, nwf_ref, o_ref,
                     x2_ref, xn_ref, acc_ref):
    v = {}
    tile = lambda t: slice(t * MXU_TILE, (t + 1) * MXU_TILE)

    def out_proj(y_ref, w_ref, t, first):
        def run():
            base = x_ref[...] if first else x2_ref[...]
            x2_ref[...] = base + _dot(y_ref[:, tile(t)], w_ref[tile(t), :])
        return run

    def gate_up(t):
        def run():
            if t == 0:
                xn_ref[...] = _rmsnorm(x2_ref[...], nw2_ref[...]).astype(BF16)
            v["g", t] = _dot(xn_ref[...], wg_ref[:, tile(t)])
            v["u", t] = _dot(xn_ref[...], wu_ref[:, tile(t)])
        return run

    def down(t):
        def run():
            hid = (_silu(v.pop(("g", t))) * v.pop(("u", t))).astype(BF16)
            part = _dot(hid, wd_ref[tile(t), :])
            acc_ref[...] = part if t == 0 else acc_ref[...] + part
        return run

    def finish():
        o_ref[...] = _rmsnorm(x2_ref[...] + 0.5 * acc_ref[...], nwf_ref[...])

    n_y = D_SSM // MXU_TILE
    n_ff = D_FF // MXU_TILE
    items = ([out_proj(ys, wos_ref, t, t == 0) for t in range(n_y)]
             + [out_proj(ya, woa_ref, t, False) for t in range(D_ATTN // MXU_TILE)])
    for t in range(n_ff):
        items.append(gate_up(t))
        if t > 0:
            items.append(down(t - 1))
    items += [down(n_ff - 1), finish]
    return items


def _mix_out_kernel(slope_ref, sink_ref, u_ref, dt_ref, z_ref, prev_ref, q_ref,
                    kvp_ref, kvn_ref, x_ref,
                    bias_ref, a_ref, dskip_ref, nw_ref,
                    wos_ref, woa_ref, nw2_ref, wg_ref, wu_ref, wd_ref, nwf_ref,
                    o_ref, ys_ref, ya_ref, st_ref, yps_ref, ypa_ref, x2_ref, xn_ref, acc_ref):
    j = pl.program_id(1)
    n_tiles = pl.num_programs(1) - 1

    @pl.when(j == 0)
    def _():
        st_ref[...] = jnp.zeros_like(st_ref)

    def output_stage_slices():
        yps_ref[...] = ys_ref[...]
        ypa_ref[...] = ya_ref[...]
        return _out_ffn2_slices(yps_ref, ypa_ref, x_ref, wos_ref, woa_ref, nw2_ref,
                                wg_ref, wu_ref, wd_ref, nwf_ref, o_ref, x2_ref, xn_ref, acc_ref)

    def mixers(fillers):
        def fill(count=1):
            for _ in range(count):
                if fillers:
                    fillers.pop(0)()

        tri_both = jnp.concatenate([_tri(lower=True), _tri(lower=False)], axis=0)
        row = lax.broadcasted_iota(jnp.int32, (CHUNK, CHUNK), 0)
        col = lax.broadcasted_iota(jnp.int32, (CHUNK, CHUNK), 1)
        a_row = _neg_exp_row(a_ref)
        n_pairs = SSM_HEADS // 2
        assert n_pairs == D_ATTN // LANES
        fill(2)
        for c in range(MIX_CHUNKS):
            pro = _ssd_prologue(c, u_ref, dt_ref, bias_ref, a_row, tri_both, col)
            fill()
            apro = _attn_prologue(c, MIX_CHUNKS, j * MIX_CHUNKS + c, n_tiles * MIX_CHUNKS - 1,
                                  (q_ref, kvp_ref, kvn_ref))
            fill()
            ydiag = []
            n_calls = n_pairs // Q_TILES_PER_CALL
            st_next = _attn_scores(c, apro, 0, q_ref)
            for i in range(n_calls):
                st = st_next
                if i + 1 < n_calls:
                    st_next = _attn_scores(c, apro, i + 1, q_ref)
                fill()
                for t in range(Q_TILES_PER_CALL):
                    ydiag.append(_ssd_head_pair(pro, Q_TILES_PER_CALL * i + t, row, col))
                    fill()
                _attn_finish(c, apro, i, st, slope_ref, sink_ref, ya_ref)
            fill(1 if c + 1 < MIX_CHUNKS else len(fillers))
            _ssd_epilogue(c, pro, ydiag, z_ref, prev_ref, dskip_ref, nw_ref, ys_ref, st_ref)
        assert not fillers

    @pl.when(j == 0)
    def _():
        mixers([])

    @pl.when(jnp.logical_and(j > 0, j < n_tiles))
    def _():
        mixers(output_stage_slices())

    @pl.when(j == n_tiles)
    def _():
        for run in output_stage_slices():
            run()


def _mix_out(u, dt, z, prevb, qkv, x1, slopes2, sink, bias, a_log, dskip, nw,
             wos, woa, nw2, wg, wu, wd, nwf):
    b, s, _ = u.shape
    n_tiles = s // MIX_TILE
    nblk = s // CHUNK
    cur = lambda j: jnp.minimum(j, n_tiles - 1)
    mix = lambda bi, j: (bi, cur(j), 0)
    lag = lambda bi, j: (bi, jnp.maximum(j - 1, 0), 0)
    kv_col = D_ATTN // (2 * D_KV)
    prv = lambda bi, j: (bi, jnp.maximum(cur(j) * MIX_CHUNKS - 1, 0), kv_col)
    nxt = lambda bi, j: (bi, jnp.minimum((cur(j) + 1) * MIX_CHUNKS, nblk - 1), kv_col)
    smem = pl.BlockSpec(memory_space=pltpu.SMEM)
    kv_prev = pl.BlockSpec((None, CHUNK, 2 * D_KV), prv)
    kv_next = pl.BlockSpec((None, CHUNK, 2 * D_KV), nxt)
    resident = (2 * (3 * D_MODEL * D_FF + (D_SSM + D_ATTN) * D_MODEL)
                + 2 * 2 * MIX_TILE * (CONV_DIM + 2 * D_SSM + D_ATTN + 2 * D_KV) + 2 * 4 * MIX_TILE * DT_PAD
                + 2 * 2 * 4 * MIX_TILE * D_MODEL + 2 * MIX_TILE * (D_SSM + D_ATTN) + 4 * D_STATE * D_SSM)
    return pl.pallas_call(
        _mix_out_kernel,
        grid=(b, n_tiles + 1),
        in_specs=[smem, smem,
                  pl.BlockSpec((None, MIX_TILE, CONV_DIM), mix),
                  pl.BlockSpec((None, MIX_TILE, DT_PAD), mix),
                  pl.BlockSpec((None, MIX_TILE, D_SSM), mix),
                  pl.BlockSpec((None, MIX_CHUNKS, D_STATE, D_SSM), lambda bi, j: (bi, cur(j), 0, 0)),
                  pl.BlockSpec((None, MIX_TILE, QKV_W), mix),
                  kv_prev, kv_next,
                  pl.BlockSpec((None, MIX_TILE, D_MODEL), lag),
                  _resident((1, DT_PAD)), _resident((1, DT_PAD)),
                  _resident((1, D_SSM)), _resident((1, D_SSM)),
                  _resident((D_SSM, D_MODEL)), _resident((D_ATTN, D_MODEL)), _resident((1, D_MODEL)),
                  _resident((D_MODEL, D_FF)), _resident((D_MODEL, D_FF)), _resident((D_FF, D_MODEL)),
                  _resident((1, D_MODEL))],
        out_specs=pl.BlockSpec((None, MIX_TILE, D_MODEL), lag),
        out_shape=jax.ShapeDtypeStruct((b, s, D_MODEL), F32),
        scratch_shapes=[pltpu.VMEM((MIX_TILE, D_SSM), BF16), pltpu.VMEM((MIX_TILE, D_ATTN), BF16),
                        pltpu.VMEM((D_STATE, D_SSM), F32),
                        pltpu.VMEM((MIX_TILE, D_SSM), BF16), pltpu.VMEM((MIX_TILE, D_ATTN), BF16),
                        pltpu.VMEM((MIX_TILE, D_MODEL), F32), pltpu.VMEM((MIX_TILE, D_MODEL), BF16),
                        pltpu.VMEM((MIX_TILE, D_MODEL), F32)],
        compiler_params=pltpu.CompilerParams(
            dimension_semantics=("arbitrary", "arbitrary"), vmem_limit_bytes=_vmem_limit(resident)),
        name="mix_out",
    )(slopes2, sink, u, dt, z, prevb, qkv, qkv, qkv, x1, bias, a_log, dskip, nw,
      wos, woa, nw2, wg, wu, wd, nwf)


def _head_selection(offset):
    e = np.zeros((2 * LANES, D_SSM), np.float32)
    for h in range(SSM_HEADS):
        e[offset + h, h * SSM_HEAD_DIM:(h + 1) * SSM_HEAD_DIM] = 1.0
        e[LANES + offset + h, h * SSM_HEAD_DIM:(h + 1) * SSM_HEAD_DIM] = 1.0
    return jnp.asarray(e, BF16)


def _prepare(norm_ffn1_w, ffn1_w_gate, ffn1_w_up, ffn1_w_down, norm_mix_w, w_in, conv_w, conv_b,
             dt_bias_fwd, dt_bias_bwd, a_log_fwd, a_log_bwd, d_skip, ssm_norm_w, attn_sink, w_out,
             norm_ffn2_w, ffn2_w_gate, ffn2_w_up, ffn2_w_down, norm_final_w):
    row = lambda v: v.reshape(1, -1).astype(F32)
    splits = np.cumsum((D_SSM, CONV_DIM, SSM_HEADS, SSM_HEADS, D_ATTN, D_KV))
    wz, wx, wdtf, wdtb, wq, wk, wv = jnp.split(w_in, splits, axis=1)
    wdt = jnp.concatenate([wdtf, wdtb, jnp.zeros((D_MODEL, DT_PAD - 2 * SSM_HEADS), F32)], axis=1)
    order = np.asarray(ATTN_HEAD_ORDER)
    cols = (order[:, None] * HEAD_DIM + np.arange(HEAD_DIM)[None, :]).reshape(-1)
    wq = wq[:, cols]
    pad_dt = jnp.zeros((DT_PAD - 2 * SSM_HEADS,), F32)
    slopes = jnp.exp2(-(8.0 / N_HEADS) * jnp.arange(1, N_HEADS + 1, dtype=F32))
    return dict(
        nw1=row(norm_ffn1_w), wg1=ffn1_w_gate.astype(BF16), wu1=ffn1_w_up.astype(BF16),
        wd1=ffn1_w_down.astype(BF16),
        nwm=row(norm_mix_w), wz=wz.astype(BF16), wx=wx.astype(BF16), wdt=wdt.astype(BF16),
        wq=wq.astype(BF16), wkv=jnp.concatenate([wk, wv], axis=1).astype(BF16),
        cw=jnp.concatenate([conv_w, jnp.zeros((SUBLANES - CONV_K, CONV_DIM), F32)], axis=0),
        cb=row(conv_b),
        dt_bias=row(jnp.concatenate([dt_bias_fwd, dt_bias_bwd, pad_dt])),
        a_log=row(jnp.concatenate([a_log_fwd, a_log_bwd, pad_dt])),
        e2b=_head_selection(SSM_HEADS),
        dskip=row(jnp.repeat(d_skip, SSM_HEAD_DIM)), nws=row(ssm_norm_w),
        slopes2=(slopes * LOG2E)[order], sink=attn_sink.astype(F32)[order],
        wos=w_out[:D_SSM].astype(BF16), woa=w_out[D_SSM:][cols].astype(BF16),
        nw2=row(norm_ffn2_w), wg2=ffn2_w_gate.astype(BF16), wu2=ffn2_w_up.astype(BF16),
        wd2=ffn2_w_down.astype(BF16), nwf=row(norm_final_w),
    )


def _trunk(x, p):
    b, s, _ = x.shape
    assert s % SEQ_TILE == 0 and (b * s) % TOK_TILE == 0 and x.shape[2] == D_MODEL
    n = b * s
    x1 = _ffn1(x.reshape(n, D_MODEL), p["nw1"], p["wg1"], p["wu1"], p["wd1"])
    z, u, dt, qkv, prevb = _mix_in(
        x1.reshape(b, s, D_MODEL), p["nwm"], p["wz"], p["wx"], p["wdt"], p["wq"], p["wkv"],
        p["cw"], p["cb"], p["dt_bias"], p["a_log"], p["e2b"])
    return _mix_out(u, dt, z, prevb, qkv, x1.reshape(b, s, D_MODEL), p["slopes2"], p["sink"],
                    p["dt_bias"], p["a_log"], p["dskip"], p["nws"],
                    p["wos"], p["woa"], p["nw2"], p["wg2"], p["wu2"], p["wd2"], p["nwf"])


def kernel(x_prompt, x_sample, norm_ffn1_w, ffn1_w_gate, ffn1_w_up, ffn1_w_down, norm_mix_w, w_in, conv_w, conv_b, dt_bias_fwd, dt_bias_bwd, a_log_fwd, a_log_bwd, d_skip, ssm_norm_w, attn_sink, w_out, norm_ffn2_w, ffn2_w_gate, ffn2_w_up, ffn2_w_down, norm_final_w):
    p = _prepare(norm_ffn1_w[0], ffn1_w_gate[0], ffn1_w_up[0], ffn1_w_down[0], norm_mix_w[0], w_in[0],
                 conv_w[0], conv_b[0], dt_bias_fwd[0], dt_bias_bwd[0], a_log_fwd[0], a_log_bwd[0],
                 d_skip[0], ssm_norm_w[0], attn_sink[0], w_out[0], norm_ffn2_w[0], ffn2_w_gate[0],
                 ffn2_w_up[0], ffn2_w_down[0], norm_final_w)
    return (_trunk(x_prompt, p), _trunk(x_sample, p))
```

```python
import math

import jax
import jax.numpy as jnp
import numpy as np
from jax import lax
from jax.experimental import pallas as pl
from jax.experimental.pallas import tpu as pltpu

F32 = jnp.float32
BF16 = jnp.bfloat16

D_MODEL = 1024
D_FF = 2816
SSM_HEADS = 16
SSM_HEAD_DIM = 64
D_SSM = SSM_HEADS * SSM_HEAD_DIM
SSM_GROUPS = 2
D_STATE = 128
CONV_K = 5
CONV_DIM = D_SSM + 2 * SSM_GROUPS * D_STATE
N_HEADS = 16
KV_HEADS = 4
HEAD_DIM = 64
D_ATTN = N_HEADS * HEAD_DIM
D_KV = KV_HEADS * HEAD_DIM
WINDOW = 128
EPS = 1e-6
LOG2E = math.log2(math.e)
Q_SCALE = LOG2E / math.sqrt(HEAD_DIM)

LANES = 128
SUBLANES = 8
VMEM_BYTES_V7X = 64 * 1024 * 1024

CHUNK = LANES
TOK_TILE = 1024
SEQ_TILE = 512
CHUNKS_PER_STEP = SEQ_TILE // CHUNK
MIX_TILE = 256
MIX_CHUNKS = MIX_TILE // CHUNK
MXU_TILE = 256
FF_SPLIT = (D_FF // MXU_TILE + 1) // 2 * MXU_TILE
FF_CHUNK = FF_SPLIT
FF_PIECE = 4 * MXU_TILE
HALO = 2 * SUBLANES
CONV_PIECE = MXU_TILE
DT_PAD = LANES
HEADS_PER_GROUP = SSM_HEADS // SSM_GROUPS
GROUP_W = HEADS_PER_GROUP * SSM_HEAD_DIM

ATTN_HEAD_ORDER = (0, 4, 1, 5, 2, 6, 3, 7, 8, 12, 9, 13, 10, 14, 11, 15)


def _vmem_limit(resident_bytes):
    return int(min(VMEM_BYTES_V7X - 8 * 1024 * 1024, 2 * resident_bytes + 8 * 1024 * 1024))


def _resident(shape):
    nd = len(shape)
    return pl.BlockSpec(shape, lambda *_: (0,) * nd, pipeline_mode=pl.Buffered(1))


def _dot(a, b):
    return jnp.dot(a, b, preferred_element_type=F32)


def _dot_nt(a, b):
    return lax.dot_general(a, b, (((1,), (1,)), ((), ())), preferred_element_type=F32)


def _dot_tn(a, b):
    return lax.dot_general(a, b, (((0,), (0,)), ((), ())), preferred_element_type=F32)


def _rmsnorm(x, w):
    return x * lax.rsqrt(jnp.mean(x * x, axis=-1, keepdims=True) + EPS) * w


def _silu(x):
    return x * jax.nn.sigmoid(x)


def _softplus(x):
    return jnp.maximum(x, 0.0) + jnp.log1p(jnp.exp(-jnp.abs(x)))


def _split2(x):
    hi = x.astype(BF16)
    lo = (x - hi.astype(F32)).astype(BF16)
    return hi, lo


def _split3(x):
    hi = x.astype(BF16)
    r = x - hi.astype(F32)
    mid = r.astype(BF16)
    lo = (r - mid.astype(F32)).astype(BF16)
    return hi, mid, lo


def _swiglu(xn, wg_ref, wu_ref, wd_ref):
    acc = None
    for cols in (slice(s, min(s + FF_PIECE, D_FF)) for s in range(0, D_FF, FF_PIECE)):
        g = _dot(xn, wg_ref[:, cols])
        u = _dot(xn, wu_ref[:, cols])
        part = _dot((_silu(g) * u).astype(BF16), wd_ref[cols, :])
        acc = part if acc is None else acc + part
    return acc


def _ffn1_kernel(x_ref, nw_ref, wg_ref, wu_ref, wd_ref, o_ref):
    x = x_ref[...]
    xn = _rmsnorm(x, nw_ref[...]).astype(BF16)
    o_ref[...] = x + 0.5 * _swiglu(xn, wg_ref, wu_ref, wd_ref)


def _ffn1(x2d, nw, wg, wu, wd):
    n = x2d.shape[0]
    tile = pl.BlockSpec((TOK_TILE, D_MODEL), lambda i: (i, 0))
    resident = 2 * 3 * D_MODEL * D_FF + 4 * 4 * TOK_TILE * D_MODEL + 3 * 4 * TOK_TILE * FF_CHUNK
    return pl.pallas_call(
        _ffn1_kernel,
        grid=(n // TOK_TILE,),
        in_specs=[tile, _resident((1, D_MODEL)), _resident((D_MODEL, D_FF)),
                  _resident((D_MODEL, D_FF)), _resident((D_FF, D_MODEL))],
        out_specs=tile,
        out_shape=jax.ShapeDtypeStruct((n, D_MODEL), F32),
        compiler_params=pltpu.CompilerParams(
            dimension_semantics=("parallel",), vmem_limit_bytes=_vmem_limit(resident)),
        name="ffn1",
    )(x2d, nw, wg, wu, wd)


def _tri(lower):
    r = lax.broadcasted_iota(jnp.int32, (CHUNK, CHUNK), 0)
    c = lax.broadcasted_iota(jnp.int32, (CHUNK, CHUNK), 1)
    return jnp.where((r >= c) if lower else (r <= c), 1.0, 0.0).astype(BF16)


def _tri_matmul(tri, x):
    hi, mid, lo = _split3(x)
    out = _dot(tri, jnp.concatenate([hi, mid, lo], axis=1))
    w = x.shape[1]
    return out[:, 0:w] + out[:, w:2 * w] + out[:, 2 * w:3 * w]


def _neg_exp_row(a_log_ref):
    lane = lax.broadcasted_iota(jnp.int32, (1, DT_PAD), 1)
    return jnp.where(lane < 2 * SSM_HEADS, -jnp.exp(a_log_ref[...]), 0.0)


def _expand_heads(x, e2_ref):
    hi, lo = _split2(x)
    return _dot(jnp.concatenate([hi, lo], axis=1), e2_ref[...])


def _mix_in_kernel(xm_ref, xp_ref, xn_ref, nw_ref, wz_ref, wx_ref, wdt_ref, wq_ref, wk_ref, wv_ref,
                   cw_ref, cb_ref, bias_ref, a_ref, e2b_ref,
                   z_ref, u_ref, dt_ref, q_ref, k_ref, v_ref, prev_ref, pad_ref, st_ref):
    j = pl.program_id(1)
    last = pl.num_programs(1) - 1

    @pl.when(j == 0)
    def _():
        st_ref[...] = jnp.zeros_like(st_ref)

    h = _rmsnorm(xm_ref[...], nw_ref[...]).astype(BF16)
    h_wide = jnp.concatenate([_rmsnorm(xp_ref[...], nw_ref[...]).astype(BF16), h,
                              _rmsnorm(xn_ref[...], nw_ref[...]).astype(BF16)], axis=0)
    dt = _dot(h, wdt_ref[...])
    dt_ref[...] = dt

    def project(w_ref, o_ref, cols, scale=None):
        def run():
            y = _dot(h, w_ref[:, cols])
            o_ref[:, cols] = (y if scale is None else y * scale).astype(BF16)
        return run

    tiles = lambda width: [slice(t * MXU_TILE, (t + 1) * MXU_TILE) for t in range(width // MXU_TILE)]
    fillers = ([project(wz_ref, z_ref, cols) for cols in tiles(D_SSM)]
               + [project(wq_ref, q_ref, cols, Q_SCALE) for cols in tiles(D_ATTN)]
               + [project(wk_ref, k_ref, cols) for cols in tiles(D_KV)]
               + [project(wv_ref, v_ref, cols) for cols in tiles(D_KV)])

    def fill(count=1):
        for _ in range(count):
            if fillers:
                fillers.pop(0)()

    first_tap = HALO - (CONV_K - 1) // 2
    n_pieces = CONV_DIM // CONV_PIECE

    def project_conv_input(piece):
        cols = slice(piece * CONV_PIECE, (piece + 1) * CONV_PIECE)
        xbc = _dot(h_wide, wx_ref[:, cols])
        pad_ref[0:HALO, cols] = jnp.where(j < last, xbc[0:HALO, :], 0.0)
        pad_ref[HALO:HALO + SEQ_TILE, cols] = xbc[HALO:HALO + SEQ_TILE, :]
        pad_ref[HALO + SEQ_TILE:2 * HALO + SEQ_TILE, cols] = jnp.where(
            j > 0, xbc[HALO + SEQ_TILE:2 * HALO + SEQ_TILE, :], 0.0)

    def conv(piece):
        cols = slice(piece * CONV_PIECE, (piece + 1) * CONV_PIECE)
        for c in range(CHUNKS_PER_STEP):
            acc = cb_ref[:, cols]
            for k in range(CONV_K):
                start = c * CHUNK + first_tap + k
                acc = acc + pad_ref[start:start + CHUNK, cols] * cw_ref[k:k + 1, cols]
            u_ref[c * CHUNK:(c + 1) * CHUNK, cols] = _silu(acc).astype(BF16)

    project_conv_input(0)
    for piece in range(n_pieces):
        if piece + 1 < n_pieces:
            project_conv_input(piece + 1)
        conv(piece)
        fill()

    tri_u = _tri(lower=False)
    a_row = _neg_exp_row(a_ref)
    chunk_rows = [slice(c * CHUNK, (c + 1) * CHUNK) for c in range(CHUNKS_PER_STEP)]
    dtvs = [_softplus(dt[r, :] + bias_ref[...]) for r in chunk_rows]
    rcss = [_tri_matmul(tri_u, dtv * a_row) for dtv in dtvs]
    fill()
    w_exps = [_expand_heads(jnp.exp(rcs[0:1, :] - rcs) * dtv, e2b_ref) for rcs, dtv in zip(rcss, dtvs)]
    decs = [_expand_heads(jnp.broadcast_to(jnp.exp(rcs[0:1, :]), (SUBLANES, LANES)), e2b_ref)[0:1, :]
            for rcs in rcss]
    fill()
    contribs = []
    for r, w_exp in zip(chunk_rows, w_exps):
        xsw = (u_ref[r, 0:D_SSM].astype(F32) * w_exp).astype(BF16)
        contribs.append([_dot_tn(u_ref[r, D_SSM + g * D_STATE:D_SSM + (g + 1) * D_STATE],
                                 xsw[:, g * GROUP_W:(g + 1) * GROUP_W]) for g in range(SSM_GROUPS)])
        fill()
    fill(len(fillers))
    for c in reversed(range(CHUNKS_PER_STEP)):
        prev_ref[c] = st_ref[...].astype(BF16)
        for g in range(SSM_GROUPS):
            cols = slice(g * GROUP_W, (g + 1) * GROUP_W)
            st_ref[:, cols] = st_ref[:, cols] * decs[c][:, cols] + contribs[c][g]


def _mix_in(x1, nw, wz, wx, wdt, wq, wk, wv, cw, cb, bias, a_log, e2b):
    b, s, _ = x1.shape
    steps = s // SEQ_TILE
    halo_per_tile = SEQ_TILE // HALO
    n_halo = s // HALO
    widths = (D_SSM, CONV_DIM, DT_PAD, D_ATTN, D_KV, D_KV)
    dtypes = (BF16, BF16, F32, BF16, BF16, BF16)
    rev = lambda bi, j: (bi, steps - 1 - j, 0)
    prv = lambda bi, j: (bi, jnp.maximum((steps - 1 - j) * halo_per_tile - 1, 0), 0)
    nxt = lambda bi, j: (bi, jnp.minimum((steps - j) * halo_per_tile, n_halo - 1), 0)
    resident = (2 * D_MODEL * sum(widths) + 2 * 2 * LANES * D_SSM
                + 2 * 4 * SEQ_TILE * (D_MODEL + sum(widths)) + 2 * 2 * SEQ_TILE * D_SSM
                + 4 * (SEQ_TILE + 2 * HALO) * CONV_DIM + 4 * D_STATE * D_SSM)
    return pl.pallas_call(
        _mix_in_kernel,
        grid=(b, steps),
        in_specs=[pl.BlockSpec((None, SEQ_TILE, D_MODEL), rev),
                  pl.BlockSpec((None, HALO, D_MODEL), prv),
                  pl.BlockSpec((None, HALO, D_MODEL), nxt),
                  _resident((1, D_MODEL))] + [_resident((D_MODEL, w)) for w in widths]
                 + [_resident((SUBLANES, CONV_DIM)), _resident((1, CONV_DIM)),
                    _resident((1, DT_PAD)), _resident((1, DT_PAD)), _resident((2 * LANES, D_SSM))],
        out_specs=[pl.BlockSpec((None, SEQ_TILE, w), rev) for w in widths]
                  + [pl.BlockSpec((None, CHUNKS_PER_STEP, D_STATE, D_SSM),
                                  lambda bi, j: (bi, steps - 1 - j, 0, 0))],
        out_shape=[jax.ShapeDtypeStruct((b, s, w), d) for w, d in zip(widths, dtypes)]
                  + [jax.ShapeDtypeStruct((b, s // CHUNK, D_STATE, D_SSM), BF16)],
        scratch_shapes=[pltpu.VMEM((SEQ_TILE + 2 * HALO, CONV_DIM), F32),
                        pltpu.VMEM((D_STATE, D_SSM), F32)],
        compiler_params=pltpu.CompilerParams(
            dimension_semantics=("arbitrary", "arbitrary"), vmem_limit_bytes=_vmem_limit(resident)),
        name="mix_in",
    )(x1, x1, x1, nw, wz, wx, wdt, wq, wk, wv, cw, cb, bias, a_log, e2b)


def _ssd_prologue(c, u_ref, dt_ref, bias_ref, a_row, tri_both, col):
    rows = slice(c * CHUNK, (c + 1) * CHUNK)
    dtv = _softplus(dt_ref[rows, :] + bias_ref[...])
    both = _tri_matmul(tri_both, dtv * a_row)
    cs = both[0:CHUNK, :]
    rcs = both[CHUNK:2 * CHUNK, :]
    comb = jnp.where(col < SSM_HEADS, cs, rcs)
    comb_t = comb.T
    dt_t = dtv.T

    xs = u_ref[rows, 0:D_SSM].astype(F32)
    bmat = [u_ref[rows, D_SSM + g * D_STATE:D_SSM + (g + 1) * D_STATE] for g in range(SSM_GROUPS)]
    cmat = [u_ref[rows, D_SSM + (SSM_GROUPS + g) * D_STATE:D_SSM + (SSM_GROUPS + g + 1) * D_STATE]
            for g in range(SSM_GROUPS)]
    cb = [_dot_nt(cmat[g], bmat[g]) for g in range(SSM_GROUPS)]
    return dict(dtv=dtv, comb=comb, comb_t=comb_t, dt_t=dt_t, xs=xs, bmat=bmat, cmat=cmat, cb=cb)


def _ssd_head_pair(pro, j, row, col):
    half = SSM_HEAD_DIM
    comb, comb_t, dt_t, dtv = pro["comb"], pro["comb_t"], pro["dt_t"], pro["dtv"]
    g = (2 * j) // HEADS_PER_GROUP
    lane_bcast = lambda a, i: jnp.broadcast_to(a[:, i:i + 1], (CHUNK, CHUNK))
    ms, wf, ef, eb, decf = [], [], [], [], []
    for h in (2 * j, 2 * j + 1):
        csf = lane_bcast(comb, h)
        rcsb = lane_bcast(comb, SSM_HEADS + h)
        df = csf - comb_t[h:h + 1, :]
        db = rcsb - comb_t[SSM_HEADS + h:SSM_HEADS + h + 1, :]
        dtf = dt_t[h:h + 1, :]
        dtb = dt_t[SSM_HEADS + h:SSM_HEADS + h + 1, :]
        dsel = jnp.where(row > col, dtf, jnp.where(row < col, dtb, dtf + dtb))
        decay = jnp.exp(jnp.where(row >= col, df, db))
        ms.append((pro["cb"][g] * decay * dsel).astype(BF16))
        last = csf[CHUNK - 1:CHUNK, :]
        wf.append(jnp.exp(last - csf) * lane_bcast(dtv, h))
        ef.append(jnp.exp(csf))
        eb.append(jnp.exp(rcsb))
        decf.append(jnp.exp(last))
    pair = lambda ab: jnp.where(col[0:ab[0].shape[0], :] < half, ab[0], ab[1])
    xt = pro["xs"][:, j * LANES:(j + 1) * LANES]
    rhs = jnp.concatenate([jnp.where(col < half, xt, 0.0), jnp.where(col >= half, xt, 0.0)],
                          axis=0).astype(BF16)
    ydiag = _dot(jnp.concatenate(ms, axis=1), rhs)
    return dict(ydiag=ydiag, wf=pair(wf), ef=pair(ef), eb=pair(eb), decf=pair(decf))


def _ssd_epilogue(c, pro, pairs, z_ref, prev_ref, dskip_ref, nw_ref, y_ref, st_ref):
    rows = slice(c * CHUNK, (c + 1) * CHUNK)
    xs, bmat, cmat = (pro[n] for n in ("xs", "bmat", "cmat"))
    ydiag, wf, ef, eb, decf = (jnp.concatenate([p[n] for p in pairs], axis=1)
                               for n in ("ydiag", "wf", "ef", "eb", "decf"))
    xsw = (xs * wf).astype(BF16)

    for g in range(SSM_GROUPS):
        cols = slice(g * GROUP_W, (g + 1) * GROUP_W)
        state = st_ref[:, cols]
        yoff_f = _dot(cmat[g], state.astype(BF16))
        yoff_b = _dot(cmat[g], prev_ref[c, :, cols])
        y = (ydiag[:, cols] + ef[:, cols] * yoff_f + eb[:, cols] * yoff_b
             + xs[:, cols] * dskip_ref[:, cols])
        y = y * _silu(z_ref[rows, cols].astype(F32))
        y = y * lax.rsqrt(jnp.mean(y * y, axis=-1, keepdims=True) + EPS) * nw_ref[:, cols]
        y_ref[rows, cols] = y.astype(BF16)
        st_ref[:, cols] = state * decf[:, cols] + _dot_tn(bmat[g], xsw[:, cols])


ATTN_KEYS = 3 * CHUNK
Q_TILES_PER_KV_TILE = (D_ATTN // LANES) // (D_KV // LANES)


def _attn_prologue(c, n_chunks, blk, last_blk, k_refs, v_refs):
    keys = ATTN_KEYS
    half = HEAD_DIM
    rows = slice(c * CHUNK, (c + 1) * CHUNK)

    srow = lax.broadcasted_iota(jnp.int32, (keys, CHUNK), 0)
    tcol = lax.broadcasted_iota(jnp.int32, (keys, CHUNK), 1)
    idist = jnp.abs(tcol + CHUNK - srow)
    lo_key = jnp.where(blk == 0, CHUNK, 0)
    hi_key = jnp.where(blk == last_blk, 2 * CHUNK, keys)
    valid = (idist <= WINDOW) & (srow >= lo_key) & (srow < hi_key)
    neg_dist = jnp.where(valid, -idist.astype(F32), -jnp.inf)
    klane = lax.broadcasted_iota(jnp.int32, (keys, LANES), 1)
    vrow = lax.broadcasted_iota(jnp.int32, (LANES, keys), 0)
    srow16 = lax.broadcasted_iota(jnp.int32, (2 * SUBLANES, 2 * keys), 0)
    scol16 = lax.broadcasted_iota(jnp.int32, (2 * SUBLANES, 2 * keys), 1)
    sum_rows = jnp.where((srow16 < SUBLANES) == (scol16 < keys), 1.0, 0.0).astype(BF16)

    def window(refs, lanes):
        main_ref, prev_ref, next_ref = refs
        prev = prev_ref[:, lanes] if c == 0 else main_ref[(c - 1) * CHUNK:c * CHUNK, lanes]
        nxt = (next_ref[:, lanes] if c == n_chunks - 1
               else main_ref[(c + 1) * CHUNK:(c + 2) * CHUNK, lanes])
        return jnp.concatenate([prev, main_ref[rows, lanes], nxt], axis=0)

    kk, vv = [], []
    for m in range(D_KV // LANES):
        lanes = slice(m * LANES, (m + 1) * LANES)
        kf = window(k_refs, lanes).astype(F32)
        kk.append(jnp.concatenate([jnp.where(klane < half, kf, 0.0), jnp.where(klane >= half, kf, 0.0)],
                                  axis=0).astype(BF16))
        vt_t = window(v_refs, lanes).astype(F32).T
        vm = jnp.concatenate([jnp.where(vrow < half, vt_t, 0.0), jnp.where(vrow >= half, vt_t, 0.0)],
                             axis=1).astype(BF16)
        vv.append(jnp.concatenate([vm, sum_rows], axis=0))
    return dict(neg_dist=neg_dist, kk=kk, vv=vv)


Q_TILES_PER_CALL = MXU_TILE // LANES
assert Q_TILES_PER_KV_TILE % Q_TILES_PER_CALL == 0


def _attn_scores(c, apro, i, q_ref):
    rows = slice(c * CHUNK, (c + 1) * CHUNK)
    kv_tile = i * Q_TILES_PER_CALL // Q_TILES_PER_KV_TILE
    q_stack = jnp.concatenate([q_ref[rows, j * LANES:(j + 1) * LANES]
                               for j in range(Q_TILES_PER_CALL * i, Q_TILES_PER_CALL * (i + 1))], axis=0)
    return _dot_nt(apro["kk"][kv_tile], q_stack)


def _attn_finish(c, apro, i, st, slope_ref, sink_ref, o_ref):
    keys = ATTN_KEYS
    half = HEAD_DIM
    rows = slice(c * CHUNK, (c + 1) * CHUNK)
    kv_tile = i * Q_TILES_PER_CALL // Q_TILES_PER_KV_TILE
    orow = lax.broadcasted_iota(jnp.int32, (LANES, CHUNK), 0)
    ps, mxs = [], []
    for t in range(Q_TILES_PER_CALL):
        p_tile = []
        for e in range(2):
            slot = 2 * (Q_TILES_PER_CALL * i + t) + e
            s = st[e * keys:(e + 1) * keys, t * CHUNK:(t + 1) * CHUNK] + slope_ref[slot] * apro["neg_dist"]
            mx = jnp.maximum(jnp.max(s, axis=0, keepdims=True), sink_ref[slot] * LOG2E)
            p_tile.append(jnp.exp2(s - mx).astype(BF16))
            mxs.append(mx)
        ps.append(jnp.concatenate(p_tile, axis=0))
    ot = _dot(apro["vv"][kv_tile], jnp.concatenate(ps, axis=1))
    for t in range(Q_TILES_PER_CALL):
        j = Q_TILES_PER_CALL * i + t
        ot_t = ot[:, t * CHUNK:(t + 1) * CHUNK]
        inv = [1.0 / (ot_t[LANES + e * SUBLANES:LANES + e * SUBLANES + 1, :]
                      + jnp.exp2(sink_ref[2 * j + e] * LOG2E - mxs[2 * t + e])) for e in range(2)]
        out = ot_t[0:LANES, :] * jnp.where(orow < half, inv[0], inv[1])
        o_ref[rows, j * LANES:(j + 1) * LANES] = out.T.astype(BF16)


def _out_ffn2_slices(ys, ya, x_ref, wos_ref, woa_ref, nw2_ref, wg_ref, wu_ref, wd_ref, nwf_ref, o_ref,
                     x2_ref, xn_ref, acc_ref):
    v = {}
    tile = lambda t: slice(t * MXU_TILE, (t + 1) * MXU_TILE)

    def out_proj(y_ref, w_ref, t, first):
        def run():
            base = x_ref[...] if first else x2_ref[...]
            x2_ref[...] = base + _dot(y_ref[:, tile(t)], w_ref[tile(t), :])
        return run

    def gate_up(t):
        def run():
            if t == 0:
                xn_ref[...] = _rmsnorm(x2_ref[...], nw2_ref[...]).astype(BF16)
            v["g", t] = _dot(xn_ref[...], wg_ref[:, tile(t)])
            v["u", t] = _dot(xn_ref[...], wu_ref[:, tile(t)])
        return run

    def down(t):
        def run():
            hid = (_silu(v.pop(("g", t))) * v.pop(("u", t))).astype(BF16)
            part = _dot(hid, wd_ref[tile(t), :])
            acc_ref[...] = part if t == 0 else acc_ref[...] + part
        return run

    def finish():
        o_ref[...] = _rmsnorm(x2_ref[...] + 0.5 * acc_ref[...], nwf_ref[...])

    n_y = D_SSM // MXU_TILE
    n_ff = D_FF // MXU_TILE
    items = ([out_proj(ys, wos_ref, t, t == 0) for t in range(n_y)]
             + [out_proj(ya, woa_ref, t, False) for t in range(D_ATTN // MXU_TILE)])
    for t in range(n_ff):
        items.append(gate_up(t))
        if t > 0:
            items.append(down(t - 1))
    items += [down(n_ff - 1), finish]
    return items


def _mix_out_kernel(slope_ref, sink_ref, u_ref, dt_ref, z_ref, prev_ref, q_ref,
                    km_ref, kp_ref, kn_ref, vm_ref, vp_ref, vn_ref, x_ref,
                    bias_ref, a_ref, dskip_ref, nw_ref,
                    wos_ref, woa_ref, nw2_ref, wg_ref, wu_ref, wd_ref, nwf_ref,
                    o_ref, ys_ref, ya_ref, st_ref, yps_ref, ypa_ref, x2_ref, xn_ref, acc_ref):
    j = pl.program_id(1)
    n_tiles = pl.num_programs(1) - 1

    @pl.when(j == 0)
    def _():
        st_ref[...] = jnp.zeros_like(st_ref)

    def output_stage_slices():
        yps_ref[...] = ys_ref[...]
        ypa_ref[...] = ya_ref[...]
        return _out_ffn2_slices(yps_ref, ypa_ref, x_ref, wos_ref, woa_ref, nw2_ref,
                                wg_ref, wu_ref, wd_ref, nwf_ref, o_ref, x2_ref, xn_ref, acc_ref)

    def mixers(fillers):
        def fill(count=1):
            for _ in range(count):
                if fillers:
                    fillers.pop(0)()

        tri_both = jnp.concatenate([_tri(lower=True), _tri(lower=False)], axis=0)
        row = lax.broadcasted_iota(jnp.int32, (CHUNK, CHUNK), 0)
        col = lax.broadcasted_iota(jnp.int32, (CHUNK, CHUNK), 1)
        a_row = _neg_exp_row(a_ref)
        n_pairs = SSM_HEADS // 2
        assert n_pairs == D_ATTN // LANES
        fill(2)
        for c in range(MIX_CHUNKS):
            pro = _ssd_prologue(c, u_ref, dt_ref, bias_ref, a_row, tri_both, col)
            fill()
            apro = _attn_prologue(c, MIX_CHUNKS, j * MIX_CHUNKS + c, n_tiles * MIX_CHUNKS - 1,
                                  (km_ref, kp_ref, kn_ref), (vm_ref, vp_ref, vn_ref))
            fill()
            ydiag = []
            n_calls = n_pairs // Q_TILES_PER_CALL
            st_next = _attn_scores(c, apro, 0, q_ref)
            for i in range(n_calls):
                st = st_next
                if i + 1 < n_calls:
                    st_next = _attn_scores(c, apro, i + 1, q_ref)
                fill()
                for t in range(Q_TILES_PER_CALL):
                    ydiag.append(_ssd_head_pair(pro, Q_TILES_PER_CALL * i + t, row, col))
                    fill()
                _attn_finish(c, apro, i, st, slope_ref, sink_ref, ya_ref)
            fill(1 if c + 1 < MIX_CHUNKS else len(fillers))
            _ssd_epilogue(c, pro, ydiag, z_ref, prev_ref, dskip_ref, nw_ref, ys_ref, st_ref)
        assert not fillers

    @pl.when(j == 0)
    def _():
        mixers([])

    @pl.when(jnp.logical_and(j > 0, j < n_tiles))
    def _():
        mixers(output_stage_slices())

    @pl.when(j == n_tiles)
    def _():
        for run in output_stage_slices():
            run()


def _mix_out(u, dt, z, prevb, q, k, v, x1, slopes2, sink, bias, a_log, dskip, nw,
             wos, woa, nw2, wg, wu, wd, nwf):
    b, s, _ = u.shape
    n_tiles = s // MIX_TILE
    nblk = s // CHUNK
    cur = lambda j: jnp.minimum(j, n_tiles - 1)
    mix = lambda bi, j: (bi, cur(j), 0)
    lag = lambda bi, j: (bi, jnp.maximum(j - 1, 0), 0)
    prv = lambda bi, j: (bi, jnp.maximum(cur(j) * MIX_CHUNKS - 1, 0), 0)
    nxt = lambda bi, j: (bi, jnp.minimum((cur(j) + 1) * MIX_CHUNKS, nblk - 1), 0)
    smem = pl.BlockSpec(memory_space=pltpu.SMEM)
    kv_main = pl.BlockSpec((None, MIX_TILE, D_KV), mix)
    kv_prev = pl.BlockSpec((None, CHUNK, D_KV), prv)
    kv_next = pl.BlockSpec((None, CHUNK, D_KV), nxt)
    resident = (2 * (3 * D_MODEL * D_FF + (D_SSM + D_ATTN) * D_MODEL)
                + 2 * 2 * MIX_TILE * (CONV_DIM + 2 * D_SSM + D_ATTN + 2 * D_KV) + 2 * 4 * MIX_TILE * DT_PAD
                + 2 * 2 * 4 * MIX_TILE * D_MODEL + 2 * MIX_TILE * (D_SSM + D_ATTN) + 4 * D_STATE * D_SSM)
    return pl.pallas_call(
        _mix_out_kernel,
        grid=(b, n_tiles + 1),
        in_specs=[smem, smem,
                  pl.BlockSpec((None, MIX_TILE, CONV_DIM), mix),
                  pl.BlockSpec((None, MIX_TILE, DT_PAD), mix),
                  pl.BlockSpec((None, MIX_TILE, D_SSM), mix),
                  pl.BlockSpec((None, MIX_CHUNKS, D_STATE, D_SSM), lambda bi, j: (bi, cur(j), 0, 0)),
                  pl.BlockSpec((None, MIX_TILE, D_ATTN), mix),
                  kv_main, kv_prev, kv_next, kv_main, kv_prev, kv_next,
                  pl.BlockSpec((None, MIX_TILE, D_MODEL), lag),
                  _resident((1, DT_PAD)), _resident((1, DT_PAD)),
                  _resident((1, D_SSM)), _resident((1, D_SSM)),
                  _resident((D_SSM, D_MODEL)), _resident((D_ATTN, D_MODEL)), _resident((1, D_MODEL)),
                  _resident((D_MODEL, D_FF)), _resident((D_MODEL, D_FF)), _resident((D_FF, D_MODEL)),
                  _resident((1, D_MODEL))],
        out_specs=pl.BlockSpec((None, MIX_TILE, D_MODEL), lag),
        out_shape=jax.ShapeDtypeStruct((b, s, D_MODEL), F32),
        scratch_shapes=[pltpu.VMEM((MIX_TILE, D_SSM), BF16), pltpu.VMEM((MIX_TILE, D_ATTN), BF16),
                        pltpu.VMEM((D_STATE, D_SSM), F32),
                        pltpu.VMEM((MIX_TILE, D_SSM), BF16), pltpu.VMEM((MIX_TILE, D_ATTN), BF16),
                        pltpu.VMEM((MIX_TILE, D_MODEL), F32), pltpu.VMEM((MIX_TILE, D_MODEL), BF16),
                        pltpu.VMEM((MIX_TILE, D_MODEL), F32)],
        compiler_params=pltpu.CompilerParams(
            dimension_semantics=("arbitrary", "arbitrary"), vmem_limit_bytes=_vmem_limit(resident)),
        name="mix_out",
    )(slopes2, sink, u, dt, z, prevb, q, k, k, k, v, v, v, x1, bias, a_log, dskip, nw,
      wos, woa, nw2, wg, wu, wd, nwf)


def _head_selection(offset):
    e = np.zeros((2 * LANES, D_SSM), np.float32)
    for h in range(SSM_HEADS):
        e[offset + h, h * SSM_HEAD_DIM:(h + 1) * SSM_HEAD_DIM] = 1.0
        e[LANES + offset + h, h * SSM_HEAD_DIM:(h + 1) * SSM_HEAD_DIM] = 1.0
    return jnp.asarray(e, BF16)


def _prepare(norm_ffn1_w, ffn1_w_gate, ffn1_w_up, ffn1_w_down, norm_mix_w, w_in, conv_w, conv_b,
             dt_bias_fwd, dt_bias_bwd, a_log_fwd, a_log_bwd, d_skip, ssm_norm_w, attn_sink, w_out,
             norm_ffn2_w, ffn2_w_gate, ffn2_w_up, ffn2_w_down, norm_final_w):
    row = lambda v: v.reshape(1, -1).astype(F32)
    splits = np.cumsum((D_SSM, CONV_DIM, SSM_HEADS, SSM_HEADS, D_ATTN, D_KV))
    wz, wx, wdtf, wdtb, wq, wk, wv = jnp.split(w_in, splits, axis=1)
    wdt = jnp.concatenate([wdtf, wdtb, jnp.zeros((D_MODEL, DT_PAD - 2 * SSM_HEADS), F32)], axis=1)
    order = np.asarray(ATTN_HEAD_ORDER)
    cols = (order[:, None] * HEAD_DIM + np.arange(HEAD_DIM)[None, :]).reshape(-1)
    wq = wq[:, cols]
    pad_dt = jnp.zeros((DT_PAD - 2 * SSM_HEADS,), F32)
    slopes = jnp.exp2(-(8.0 / N_HEADS) * jnp.arange(1, N_HEADS + 1, dtype=F32))
    return dict(
        nw1=row(norm_ffn1_w), wg1=ffn1_w_gate.astype(BF16), wu1=ffn1_w_up.astype(BF16),
        wd1=ffn1_w_down.astype(BF16),
        nwm=row(norm_mix_w), wz=wz.astype(BF16), wx=wx.astype(BF16), wdt=wdt.astype(BF16),
        wq=wq.astype(BF16), wk=wk.astype(BF16), wv=wv.astype(BF16),
        cw=jnp.concatenate([conv_w, jnp.zeros((SUBLANES - CONV_K, CONV_DIM), F32)], axis=0),
        cb=row(conv_b),
        dt_bias=row(jnp.concatenate([dt_bias_fwd, dt_bias_bwd, pad_dt])),
        a_log=row(jnp.concatenate([a_log_fwd, a_log_bwd, pad_dt])),
        e2b=_head_selection(SSM_HEADS),
        dskip=row(jnp.repeat(d_skip, SSM_HEAD_DIM)), nws=row(ssm_norm_w),
        slopes2=(slopes * LOG2E)[order], sink=attn_sink.astype(F32)[order],
        wos=w_out[:D_SSM].astype(BF16), woa=w_out[D_SSM:][cols].astype(BF16),
        nw2=row(norm_ffn2_w), wg2=ffn2_w_gate.astype(BF16), wu2=ffn2_w_up.astype(BF16),
        wd2=ffn2_w_down.astype(BF16), nwf=row(norm_final_w),
    )


def _trunk(x, p):
    b, s, _ = x.shape
    assert s % SEQ_TILE == 0 and (b * s) % TOK_TILE == 0 and x.shape[2] == D_MODEL
    n = b * s
    x1 = _ffn1(x.reshape(n, D_MODEL), p["nw1"], p["wg1"], p["wu1"], p["wd1"])
    z, u, dt, q, k, v, prevb = _mix_in(
        x1.reshape(b, s, D_MODEL), p["nwm"], p["wz"], p["wx"], p["wdt"], p["wq"], p["wk"], p["wv"],
        p["cw"], p["cb"], p["dt_bias"], p["a_log"], p["e2b"])
    return _mix_out(u, dt, z, prevb, q, k, v, x1.reshape(b, s, D_MODEL), p["slopes2"], p["sink"],
                    p["dt_bias"], p["a_log"], p["dskip"], p["nws"],
                    p["wos"], p["woa"], p["nw2"], p["wg2"], p["wu2"], p["wd2"], p["nwf"])


def kernel(x_prompt, x_sample, norm_ffn1_w, ffn1_w_gate, ffn1_w_up, ffn1_w_down, norm_mix_w, w_in, conv_w, conv_b, dt_bias_fwd, dt_bias_bwd, a_log_fwd, a_log_bwd, d_skip, ssm_norm_w, attn_sink, w_out, norm_ffn2_w, ffn2_w_gate, ffn2_w_up, ffn2_w_down, norm_final_w):
    p = _prepare(norm_ffn1_w[0], ffn1_w_gate[0], ffn1_w_up[0], ffn1_w_down[0], norm_mix_w[0], w_in[0],
                 conv_w[0], conv_b[0], dt_bias_fwd[0], dt_bias_bwd[0], a_log_fwd[0], a_log_bwd[0],
                 d_skip[0], ssm_norm_w[0], attn_sink[0], w_out[0], norm_ffn2_w[0], ffn2_w_gate[0],
                 ffn2_w_up[0], ffn2_w_down[0], norm_final_w)
    return (_trunk(x_prompt, p), _trunk(x_sample, p))
```
